```python
import jax
import jax.numpy as jnp
from jax import lax
import numpy as np

D_MODEL = 1024
BATCH = 4
SEQ = 4096
DEPTH = 2

CHUNK = 64
PLE_DIM = 256
HEAD_DIM = 64
D_BRANCH = D_MODEL // 2
N_HEADS = D_BRANCH // HEAD_DIM
N_BRANCH = 3
NORM_EPS = 1e-6
LORA_W = 64
LORA_A = 64
RWKV_GN_EPS = 64e-5
IDX_HEADS = 4
IDX_DIM = 64
TOPK_MAX = 256
Q_BLOCK = 128
ROPE_THETA = 500000.0
ROPE_DIM = HEAD_DIM // 4
CONV_WIDTH = 4
LRU_C = 8.0

SPLITS_A = (D_BRANCH, D_BRANCH, D_BRANCH, LORA_W, LORA_A, D_BRANCH)
SPLITS_B = (D_BRANCH, D_BRANCH, D_BRANCH, IDX_HEADS * IDX_DIM, IDX_DIM, IDX_HEADS, D_BRANCH)
SPLITS_C = (D_BRANCH, D_BRANCH)
D_A_IN = sum(SPLITS_A)
D_B_IN = sum(SPLITS_B)
D_C_IN = sum(SPLITS_C)
D_GATE_IN = N_BRANCH * D_MODEL
D_IN = D_A_IN + D_B_IN + D_C_IN + D_GATE_IN

kernel_name = 'hybrid_rwkv7_dsa_rglru_gated_encoder'

F32 = jnp.float32


def split_cols(u, sizes):
    offsets = np.cumsum(np.array(sizes))[:-1].tolist()
    return jnp.split(u, offsets, axis=-1)


def rmsnorm(x, g):
    xf = x.astype(F32)
    y = xf * lax.rsqrt(jnp.mean(xf * xf, axis=-1, keepdims=True) + NORM_EPS)
    return (y * g.astype(F32)).astype(x.dtype)


def rope_tables(positions):
    inv_freq = ROPE_THETA ** (-jnp.arange(0, ROPE_DIM, 2, dtype=F32) / ROPE_DIM)
    ang = positions.astype(F32)[..., None] * inv_freq
    return jnp.cos(ang), jnp.sin(ang)


def partial_rope(x, cos, sin):
    half = ROPE_DIM // 2
    x1 = x[..., :half].astype(F32)
    x2 = x[..., half:ROPE_DIM].astype(F32)
    rot = jnp.concatenate([x1 * cos - x2 * sin, x2 * cos + x1 * sin], axis=-1)
    return jnp.concatenate([rot.astype(x.dtype), x[..., ROPE_DIM:]], axis=-1)


def token_shift(u, mu):
    prev = jnp.pad(u, ((0, 0), (1, 0), (0, 0)))[:, :-1]
    return u + (prev - u) * mu


def rwkv7_scan(r, decay, k, v, kk, a):
    B, S, H, N = r.shape

    def step(state, inp):
        r_t, w_t, k_t, v_t, kk_t, a_t = inp
        sa = jnp.einsum('bhij,bhj->bhi', state, -kk_t)
        state = (state * w_t[:, :, None, :]
                 + sa[..., None] * (kk_t * a_t)[:, :, None, :]
                 + v_t[..., None] * k_t[:, :, None, :])
        return state, jnp.einsum('bhij,bhj->bhi', state, r_t)

    xs = tuple(jnp.moveaxis(t, 1, 0) for t in (r, decay, k, v, kk, a))
    _, out = lax.scan(step, jnp.zeros((B, H, N, N), F32), xs)
    return jnp.moveaxis(out, 0, 1)


def rwkv7_branch(u, mu, w0, w2, a0, a2, k_k, k_a, r_k, gn_g, gn_b):
    B, S, _ = u.shape
    r, k, v, wl, al, g = split_cols(token_shift(u, mu), SPLITS_A)
    heads = lambda t: t.reshape(B, S, N_HEADS, HEAD_DIM)
    w_log = -jax.nn.softplus(-(w0 + jnp.tanh(wl) @ w2).astype(F32)) - 0.5
    decay = jnp.exp(-jnp.exp(w_log))
    a = jax.nn.sigmoid((a0 + al @ a2).astype(F32))
    kk = heads(k.astype(F32) * k_k.astype(F32))
    kk = kk / jnp.maximum(jnp.sqrt(jnp.sum(kk * kk, axis=-1, keepdims=True)), 1e-12)
    k_mod = k.astype(F32) * (1.0 + (a - 1.0) * k_a.astype(F32))
    rh, kh, vh = heads(r.astype(F32)), heads(k_mod), heads(v.astype(F32))
    o = rwkv7_scan(rh, heads(decay), kh, vh, kk, heads(a))
    mean = jnp.mean(o, axis=-1, keepdims=True)
    var = jnp.mean(jnp.square(o - mean), axis=-1, keepdims=True)
    o = ((o - mean) * lax.rsqrt(var + RWKV_GN_EPS)).reshape(B, S, D_BRANCH)
    o = o * gn_g.astype(F32) + gn_b.astype(F32)
    bonus = jnp.sum(rh * kh * r_k.astype(F32), axis=-1, keepdims=True) * vh
    y = o + bonus.reshape(B, S, D_BRANCH)
    return y.astype(u.dtype) * jax.nn.silu(g)


def dsa_attention(q, k, v, qi, ki, wi, k_sel):
    B, S, H, D = q.shape
    n_blocks = S // Q_BLOCK
    chunk_of = jnp.arange(S) // CHUNK
    scale = HEAD_DIM ** -0.5

    def block(bi):
        start = bi * Q_BLOCK
        qb = lax.dynamic_slice_in_dim(q, start, Q_BLOCK, axis=1)
        qib = lax.dynamic_slice_in_dim(qi, start, Q_BLOCK, axis=1)
        wib = lax.dynamic_slice_in_dim(wi, start, Q_BLOCK, axis=1)
        q_chunk = lax.dynamic_slice_in_dim(chunk_of, start, Q_BLOCK, axis=0)
        dots = jnp.einsum('bqhd,bsd->bqhs', qib, ki).astype(F32)
        score = jnp.einsum('bqh,bqhs->bqs', wib.astype(F32), jax.nn.relu(dots))
        admissible = chunk_of[None, :] <= q_chunk[:, None]
        score = jnp.where(admissible[None], score, -jnp.inf)
        _, idx = lax.top_k(score, k_sel)
        valid = chunk_of[idx] <= q_chunk[None, :, None]
        ks = jax.vmap(lambda kb, ib: kb[ib])(k, idx)
        vs = jax.vmap(lambda vb, ib: vb[ib])(v, idx)
        logits = jnp.einsum('bqhd,bqkhd->bqhk', qb, ks).astype(F32) * scale
        logits = jnp.where(valid[:, :, None, :], logits, -jnp.inf)
        prob = jax.nn.softmax(logits, axis=-1).astype(v.dtype)
        return jnp.einsum('bqhk,bqkhd->bqhd', prob, vs)

    out = lax.map(block, jnp.arange(n_blocks))
    return jnp.moveaxis(out, 0, 1).reshape(B, S, H * D)


def dsa_branch(u, cos, sin, q_g, k_g, k_sel):
    B, S, _ = u.shape
    q, k, v, qi, ki, wi, g = split_cols(u, SPLITS_B)
    heads = lambda t: t.reshape(B, S, N_HEADS, HEAD_DIM)
    cos_h, sin_h = cos[:, :, None, :], sin[:, :, None, :]
    q = partial_rope(rmsnorm(heads(q), q_g), cos_h, sin_h)
    k = partial_rope(rmsnorm(heads(k), k_g), cos_h, sin_h)
    qi = partial_rope(qi.reshape(B, S, IDX_HEADS, IDX_DIM), cos_h, sin_h)
    ki = partial_rope(ki, cos, sin)
    wi = wi * (IDX_HEADS ** -0.5 * IDX_DIM ** -0.5)
    o = dsa_attention(q, k, heads(v), qi, ki, wi, k_sel)
    return o * jax.nn.silu(g)


def causal_depthwise_conv(x, w, b):
    y = lax.conv_general_dilated(
        x, w[:, None, :].astype(x.dtype), window_strides=(1,),
        padding=[(CONV_WIDTH - 1, 0)],
        dimension_numbers=('NWC', 'WIO', 'NWC'),
        feature_group_count=x.shape[-1])
    return y + b


def rglru_branch(u, conv_w, conv_b, w_r, b_r, w_i, b_i, lam):
    B, S, _ = u.shape
    x, g = split_cols(u, SPLITS_C)
    xc = causal_depthwise_conv(x, conv_w, conv_b)
    xh = xc.reshape(B, S, N_HEADS, HEAD_DIM)
    r = jax.nn.sigmoid((jnp.einsum('bshi,hij->bshj', xh, w_r).reshape(B, S, D_BRANCH) + b_r).astype(F32))
    i = jax.nn.sigmoid((jnp.einsum('bshi,hij->bshj', xh, w_i).reshape(B, S, D_BRANCH) + b_i).astype(F32))
    log_a = -LRU_C * r * jax.nn.softplus(-lam.astype(F32))
    a = jnp.exp(log_a)
    b = jnp.sqrt(-jnp.expm1(2.0 * log_a)) * (i * xc.astype(F32))

    def combine(c1, c2):
        a1, b1 = c1
        a2, b2 = c2
        return a1 * a2, a2 * b1 + b2

    _, h = lax.associative_scan(combine, (a, b), axis=1)
    return h.astype(u.dtype) * jax.nn.silu(g)


def setup_inputs(seed: int = 0) -> dict:
    key = jax.random.key(seed)
    ks = jax.random.split(key, 32)
    nrm = lambda k, shape, s: jax.random.normal(k, shape, F32) * s
    a8 = jax.random.uniform(ks[22], (DEPTH, D_BRANCH), F32, 0.9, 0.999)
    a_base = a8 ** (1.0 / LRU_C)
    return {
        'x': nrm(ks[0], (BATCH, SEQ, D_MODEL), 1.0),
        'p': nrm(ks[1], (DEPTH, BATCH, SEQ, PLE_DIM), 1.0),
        'positions': jax.random.randint(ks[2], (BATCH, 1), 0, 2048, jnp.int32) + jnp.arange(SEQ, dtype=jnp.int32)[None, :],
        'norm_g': 1.0 + nrm(ks[3], (DEPTH, D_MODEL), 0.02),
        'w_in': nrm(ks[4], (DEPTH, D_MODEL, D_IN), D_MODEL ** -0.5),
        'rwkv_mu': jax.random.uniform(ks[5], (DEPTH, D_A_IN), F32, 0.0, 1.0),
        'rwkv_w0': jax.random.uniform(ks[6], (DEPTH, D_BRANCH), F32, -6.0, 1.0),
        'rwkv_w2': nrm(ks[7], (DEPTH, LORA_W, D_BRANCH), 0.1 * LORA_W ** -0.5),
        'rwkv_a0': nrm(ks[8], (DEPTH, D_BRANCH), 0.1),
        'rwkv_a2': nrm(ks[9], (DEPTH, LORA_A, D_BRANCH), 0.5 * LORA_A ** -0.5),
        'rwkv_k_k': 0.85 + nrm(ks[10], (DEPTH, D_BRANCH), 0.05),
        'rwkv_k_a': 1.0 + nrm(ks[11], (DEPTH, D_BRANCH), 0.05),
        'rwkv_r_k': nrm(ks[12], (DEPTH, N_HEADS, HEAD_DIM), 0.1),
        'rwkv_gn_g': 1.0 + nrm(ks[13], (DEPTH, D_BRANCH), 0.02),
        'rwkv_gn_b': nrm(ks[14], (DEPTH, D_BRANCH), 0.01),
        'dsa_q_g': 1.0 + nrm(ks[15], (DEPTH, HEAD_DIM), 0.02),
        'dsa_k_g': 1.0 + nrm(ks[16], (DEPTH, HEAD_DIM), 0.02),
        'lru_conv_w': nrm(ks[17], (DEPTH, CONV_WIDTH, D_BRANCH), CONV_WIDTH ** -0.5),
        'lru_conv_b': nrm(ks[18], (DEPTH, D_BRANCH), 0.01),
        'lru_w_r': nrm(ks[19], (DEPTH, N_HEADS, HEAD_DIM, HEAD_DIM), HEAD_DIM ** -0.5),
        'lru_b_r': nrm(ks[20], (DEPTH, D_BRANCH), 0.01),
        'lru_w_i': nrm(ks[21], (DEPTH, N_HEADS, HEAD_DIM, HEAD_DIM), HEAD_DIM ** -0.5),
        'lru_b_i': nrm(ks[23], (DEPTH, D_BRANCH), 0.01),
        'lru_lambda': jnp.log(a_base) - jnp.log1p(-a_base),
        'w_branch': nrm(ks[24], (DEPTH, N_BRANCH, D_BRANCH, D_MODEL), D_BRANCH ** -0.5),
        'w_out': nrm(ks[25], (DEPTH, D_MODEL, D_MODEL), D_MODEL ** -0.5),
        'w_ple': nrm(ks[26], (DEPTH, PLE_DIM, D_MODEL), PLE_DIM ** -0.5),
        'w_ple_gate': nrm(ks[27], (DEPTH, D_MODEL, D_MODEL), D_MODEL ** -0.5),
    }


def reference(x, p, positions, norm_g, w_in, rwkv_mu, rwkv_w0, rwkv_w2, rwkv_a0, rwkv_a2,
              rwkv_k_k, rwkv_k_a, rwkv_r_k, rwkv_gn_g, rwkv_gn_b, dsa_q_g, dsa_k_g,
              lru_conv_w, lru_conv_b, lru_w_r, lru_b_r, lru_w_i, lru_b_i, lru_lambda,
              w_branch, w_out, w_ple, w_ple_gate):
    B, S, _ = x.shape
    k_sel = min(TOPK_MAX, S // 4)
    cos, sin = rope_tables(positions)
    h = x
    for i in range(DEPTH):
        hn = rmsnorm(h, norm_g[i])
        u = hn @ w_in[i]
        u_a, u_b, u_c, u_g = split_cols(u, (D_A_IN, D_B_IN, D_C_IN, D_GATE_IN))
        y_a = rwkv7_branch(u_a, rwkv_mu[i], rwkv_w0[i], rwkv_w2[i], rwkv_a0[i], rwkv_a2[i],
                           rwkv_k_k[i], rwkv_k_a[i], rwkv_r_k[i], rwkv_gn_g[i], rwkv_gn_b[i])
        y_b = dsa_branch(u_b, cos, sin, dsa_q_g[i], dsa_k_g[i], k_sel)
        y_c = rglru_branch(u_c, lru_conv_w[i], lru_conv_b[i], lru_w_r[i], lru_b_r[i],
                           lru_w_i[i], lru_b_i[i], lru_lambda[i])
        ys = jnp.stack([y_a, y_b, y_c], axis=2)
        y_proj = jnp.einsum('bsnc,ncd->bsnd', ys, w_branch[i])
        gates = jax.nn.sigmoid(u_g.reshape(B, S, N_BRANCH, D_MODEL))
        merged = jnp.sum(gates * y_proj, axis=2)
        h = h + merged @ w_out[i]
        h = h + jax.nn.sigmoid(h @ w_ple_gate[i]) * (p[i] @ w_ple[i])
    return h
```

```python
import functools

import numpy as np
import jax
import jax.numpy as jnp
from jax import lax
from jax.experimental import pallas as pl
from jax.experimental.pallas import tpu as pltpu

F32 = jnp.float32
BF16 = jnp.bfloat16
I32 = jnp.int32

D_MODEL = 1024
D_BRANCH = 512
HEAD_DIM = 64
N_HEADS = 8
N_BRANCH = 3
PLE_DIM = 256
CHUNK = 64
CHUNK_SHIFT = 6
NORM_EPS = 1e-6
LORA = 64
RWKV_GN_EPS = 64e-5
IDX_HEADS = 4
IDX_DIM = 64
TOPK_MAX = 256
ROPE_THETA = 500000.0
ROPE_DIM = 16
CONV_WIDTH = 4
LRU_C = 8.0

LANES = 128
VMEM_LIMIT = 48 * 1024 * 1024

W_G = N_BRANCH * D_MODEL
W_C = 2 * D_BRANCH
W_BM = 4 * D_BRANCH
W_BI = 384
W_A = 4 * D_BRANCH + 2 * LORA
OFF_G, OFF_C, OFF_BM, OFF_BI, OFF_A = 0, 3072, 4096, 6144, 6528
D_IN_PAD = OFF_A + W_A

NEG_BIG = -1e30
INT_MIN = -2 ** 31


def _dot(a, b):
    return jnp.dot(a.astype(BF16), b.astype(BF16), preferred_element_type=F32)


def _dot_nt(a, b):
    return lax.dot_general(a.astype(BF16), b.astype(BF16), (((1,), (1,)), ((), ())),
                           preferred_element_type=F32)


def _dot_f32(a, b):
    return jnp.dot(a, b, precision=lax.Precision.HIGHEST, preferred_element_type=F32)


def _dot_f32_nt(a, b):
    return lax.dot_general(a, b, (((1,), (1,)), ((), ())), precision=lax.Precision.HIGHEST,
                           preferred_element_type=F32)


def _dot_f32_tn(a, b):
    return lax.dot_general(a, b, (((0,), (0,)), ((), ())), precision=lax.Precision.HIGHEST,
                           preferred_element_type=F32)


def _dot_sel(x, sel_bf16):
    hi = x.astype(BF16)
    r1 = x - hi.astype(F32)
    mid = r1.astype(BF16)
    lo = (r1 - mid.astype(F32)).astype(BF16)
    out = jnp.dot(hi, sel_bf16, preferred_element_type=F32)
    out = out + jnp.dot(mid, sel_bf16, preferred_element_type=F32)
    return out + jnp.dot(lo, sel_bf16, preferred_element_type=F32)


def _params(*sem):
    return pltpu.CompilerParams(dimension_semantics=sem, vmem_limit_bytes=VMEM_LIMIT)


def _inproj_body(h_ref, g_ref, w_ref, o_ref, hn_ref):
    @pl.when(pl.program_id(1) == 0)
    def _():
        x = h_ref[...]
        ms = jnp.mean(x * x, axis=-1, keepdims=True)
        hn_ref[...] = (x * lax.rsqrt(ms + NORM_EPS) * g_ref[...]).astype(BF16)

    o_ref[...] = jnp.dot(hn_ref[...], w_ref[...], preferred_element_type=F32)


def _inproj(h, g, w):
    T, D = h.shape
    N = w.shape[1]
    tm, tn = min(1024, T), 512
    return pl.pallas_call(
        _inproj_body,
        grid=(T // tm, N // tn),
        in_specs=[pl.BlockSpec((tm, D), lambda i, j: (i, 0)),
                  pl.BlockSpec((1, D), lambda i, j: (0, 0)),
                  pl.BlockSpec((D, tn), lambda i, j: (0, j))],
        out_specs=pl.BlockSpec((tm, tn), lambda i, j: (i, j)),
        out_shape=jax.ShapeDtypeStruct((T, N), F32),
        scratch_shapes=[pltpu.VMEM((tm, D), BF16)],
        compiler_params=_params("parallel", "arbitrary"),
        name="inproj",
    )(h, g, w)


def _rwkv_body(ua_ref, mu_ref, w0_ref, w2_ref, a0_ref, a2_ref, kk_ref, ka_ref, rk_ref,
               gng_ref, gnb_ref, hsel_ref, o_ref, z_ref, prev_ref, obuf_ref):
    C = ua_ref.shape[0]
    DB = D_BRANCH

    @pl.when(pl.program_id(1) == 0)
    def _():
        z_ref[...] = jnp.zeros_like(z_ref)
        prev_ref[...] = jnp.zeros_like(prev_ref)

    ua = ua_ref[...]
    row = lax.broadcasted_iota(I32, (C, 1), 0)
    shifted = jnp.where(row == 0, prev_ref[...], pltpu.roll(ua, 1, 0))
    prev_ref[...] = ua[C - 1:C, :]
    xs = ua + (shifted - ua) * mu_ref[...]

    r = xs[:, 0:DB]
    k = xs[:, DB:2 * DB]
    v = xs[:, 2 * DB:3 * DB]
    wl = xs[:, 3 * DB:3 * DB + LORA]
    al = xs[:, 3 * DB + LORA:3 * DB + 2 * LORA]
    g = xs[:, 3 * DB + 2 * LORA:]
    hsel = hsel_ref[...]

    w_pre = w0_ref[...] + _dot(jnp.tanh(wl), w2_ref[...])
    lw = -jnp.exp(-jax.nn.softplus(-w_pre) - 0.5)
    a = jax.nn.sigmoid(a0_ref[...] + _dot(al, a2_ref[...]))
    kk = k * kk_ref[...]
    kk = kk / jnp.maximum(jnp.sqrt(_dot_sel(kk * kk, hsel)), 1e-12)
    k_mod = k * (1.0 + (a - 1.0) * ka_ref[...])
    b = kk * a

    ci = lax.broadcasted_iota(I32, (C, C), 0)
    cj = lax.broadcasted_iota(I32, (C, C), 1)
    incl = ci >= cj
    strict = ci > cj
    eye = ci == cj
    cl = _dot_sel_lhs(jnp.where(incl, 1.0, 0.0).astype(BF16), lw)
    cl_end = cl[C - 1:C, :]
    g_incl = jnp.exp(cl)
    g_inv = jnp.exp(-cl)
    d_end = jnp.exp(cl_end - cl)
    kt_all = kk * jnp.exp(cl - lw)
    rt_all = r * g_incl
    bs_all = b * g_inv
    ks_all = k_mod * g_inv
    bh_all = b * d_end
    kh_all = k_mod * d_end
    g_end = g_incl[C - 1:C, :]

    for h in range(N_HEADS):
        hs = slice(h * HEAD_DIM, (h + 1) * HEAD_DIM)
        kt, rt, bs, ks, bh, kh, vh = (t[:, hs] for t in
                                      (kt_all, rt_all, bs_all, ks_all, bh_all, kh_all, v))
        a_m = jnp.where(strict, _dot_f32_nt(kt, bs), 0.0)
        b_m = jnp.where(strict, _dot_f32_nt(kt, ks), 0.0)
        ab = jnp.where(incl, _dot_f32_nt(rt, bs), 0.0)
        bk = jnp.where(incl, _dot_f32_nt(rt, ks), 0.0)
        ident = jnp.where(eye, 1.0, 0.0)
        pw = -a_m
        tinv = ident + pw
        n_sq = int(np.log2(C)) - 1
        for _ in range(n_sq):
            pw = _dot_f32(pw, pw)
            tinv = _dot_f32(tinv, ident + pw)
        w_m = -_dot_f32(tinv, kt)
        u0 = -_dot_f32(tinv, _dot_f32(b_m, vh))
        m_m = jnp.where(eye, g_end[:, hs], 0.0) + _dot_f32_tn(bh, w_m)
        n_m = _dot_f32_tn(bh, u0) + _dot_f32_tn(kh, vh)
        qe = rt + _dot_f32(ab, w_m)
        o0 = _dot_f32(ab, u0) + _dot_f32(bk, vh)
        z = z_ref[h]
        obuf_ref[:, hs] = _dot_f32(qe, z) + o0
        z_ref[h] = _dot_f32(m_m, z) + n_m

    o = obuf_ref[...]
    inv_n = 1.0 / HEAD_DIM
    mean = _dot_sel(o, hsel) * inv_n
    oc = o - mean
    var = _dot_sel(oc * oc, hsel) * inv_n
    o = oc * lax.rsqrt(var + RWKV_GN_EPS) * gng_ref[...] + gnb_ref[...]
    bonus = _dot_sel(r * k_mod * rk_ref[...], hsel) * v
    o_ref[...] = ((o + bonus) * (g * jax.nn.sigmoid(g))).astype(o_ref.dtype)


def _dot_sel_lhs(sel_bf16, x):
    hi = x.astype(BF16)
    r1 = x - hi.astype(F32)
    mid = r1.astype(BF16)
    lo = (r1 - mid.astype(F32)).astype(BF16)
    out = jnp.dot(sel_bf16, hi, preferred_element_type=F32)
    out = out + jnp.dot(sel_bf16, mid, preferred_element_type=F32)
    return out + jnp.dot(sel_bf16, lo, preferred_element_type=F32)


def _rwkv(u, B, S, mu, w0, w2, a0, a2, k_k, k_a, r_k, gn_g, gn_b, hsel):
    T = u.shape[0]
    C = CHUNK
    nc = S // C
    a_blk = OFF_A // W_A
    vec = lambda n: pl.BlockSpec((1, n), lambda b, c: (0, 0))
    mat = lambda m, n: pl.BlockSpec((m, n), lambda b, c: (0, 0))
    return pl.pallas_call(
        _rwkv_body,
        grid=(B, nc),
        in_specs=[pl.BlockSpec((C, W_A), lambda b, c: (b * nc + c, a_blk)),
                  vec(W_A), vec(D_BRANCH), mat(LORA, D_BRANCH), vec(D_BRANCH), mat(LORA, D_BRANCH),
                  vec(D_BRANCH), vec(D_BRANCH), vec(D_BRANCH), vec(D_BRANCH), vec(D_BRANCH),
                  mat(D_BRANCH, D_BRANCH)],
        out_specs=pl.BlockSpec((C, D_BRANCH), lambda b, c: (b * nc + c, 0)),
        out_shape=jax.ShapeDtypeStruct((T, D_BRANCH), BF16),
        scratch_shapes=[pltpu.VMEM((N_HEADS, HEAD_DIM, HEAD_DIM), F32),
                        pltpu.VMEM((1, W_A), F32),
                        pltpu.VMEM((C, D_BRANCH), F32)],
        compiler_params=_params("parallel", "arbitrary"),
        name="rwkv",
    )(u, mu, w0, w2, a0, a2, k_k, k_a, r_k, gn_g, gn_b, hsel)


def _rope(x, cosw, sinw):
    W = x.shape[1]
    lane = lax.broadcasted_iota(I32, (1, W), 1) & (HEAD_DIM - 1)
    half = ROPE_DIM // 2
    partner = jnp.where(lane < half, pltpu.roll(x, W - half, 1), pltpu.roll(x, half, 1))
    return x * cosw + partner * sinw


def _dsa_prep_body(um_ref, ui_ref, cos_ref, sin_ref, qg_ref, kg_ref, hsel_ref,
                   qn_ref, kn_ref, vb_ref, qi_ref, ki_ref, wi_ref):
    DB = D_BRANCH
    cos1, sin1 = cos_ref[...], sin_ref[...]
    cos2, sin2 = jnp.concatenate([cos1] * 2, axis=1), jnp.concatenate([sin1] * 2, axis=1)
    cos4, sin4 = jnp.concatenate([cos2] * 2, axis=1), jnp.concatenate([sin2] * 2, axis=1)
    hsel = hsel_ref[...]

    def headnorm(x, gain):
        ms = _dot_sel(x * x, hsel) * (1.0 / HEAD_DIM)
        return x * lax.rsqrt(ms + NORM_EPS) * gain

    q = _rope(headnorm(um_ref[:, 0:DB], qg_ref[...]), cos4, sin4)
    k = _rope(headnorm(um_ref[:, DB:2 * DB], kg_ref[...]), cos4, sin4)
    qn_ref[...] = (q * (HEAD_DIM ** -0.5)).astype(BF16)
    kn_ref[...] = k.astype(BF16)
    vb_ref[...] = um_ref[:, 2 * DB:3 * DB].astype(BF16)
    qi_ref[...] = _rope(ui_ref[:, 0:IDX_HEADS * IDX_DIM], cos2, sin2).astype(BF16)
    kw = ui_ref[:, IDX_HEADS * IDX_DIM:]
    ki_ref[...] = _rope(kw, cos1, sin1)[:, 0:IDX_DIM].astype(BF16)
    wi_ref[...] = kw * (IDX_HEADS ** -0.5 * IDX_DIM ** -0.5)


def _dsa_prep(u, cosf, sinf, qg, kg, hsel):
    T = u.shape[0]
    tm = min(512, T)
    vec = lambda n: pl.BlockSpec((1, n), lambda i: (0, 0))
    row = lambda n: pl.BlockSpec((tm, n), lambda i: (i, 0))
    return pl.pallas_call(
        _dsa_prep_body,
        grid=(T // tm,),
        in_specs=[pl.BlockSpec((tm, W_BM), lambda i: (i, OFF_BM // W_BM)),
                  pl.BlockSpec((tm, W_BI), lambda i: (i, OFF_BI // W_BI)),
                  row(LANES), row(LANES), vec(D_BRANCH), vec(D_BRANCH),
                  pl.BlockSpec((D_BRANCH, D_BRANCH), lambda i: (0, 0))],
        out_specs=[row(D_BRANCH), row(D_BRANCH), row(D_BRANCH), row(IDX_HEADS * IDX_DIM),
                   row(IDX_DIM), row(LANES)],
        out_shape=[jax.ShapeDtypeStruct((T, D_BRANCH), BF16)] * 3
        + [jax.ShapeDtypeStruct((T, IDX_HEADS * IDX_DIM), BF16),
           jax.ShapeDtypeStruct((T, IDX_DIM), BF16),
           jax.ShapeDtypeStruct((T, LANES), F32)],
        compiler_params=_params("parallel"),
        name="dsa_prep",
    )(u, u, cosf, sinf, qg, kg, hsel)


def _dsa_body(qn_ref, qi_ref, wi_ref, g_ref, kn_ref, v_ref, ki_ref, o_ref,
              key_ref, m_ref, l_ref, acc_ref, *, ksel, kt):
    QB = qn_ref.shape[0]
    S = kn_ref.shape[0]
    q0 = pl.program_id(1) * QB
    row = lax.broadcasted_iota(I32, (QB, 1), 0)
    lim = (((q0 + row) >> CHUNK_SHIFT) + 1) << CHUNK_SHIFT
    n_t = (q0 + QB + kt - 1) // kt
    lane = lax.broadcasted_iota(I32, (1, kt), 1)
    imin = jnp.int32(INT_MIN)
    qi = qi_ref[...]
    wi = wi_ref[...]

    def score_tile(t, carry):
        off = pl.multiple_of(t * kt, kt)
        ki_t = ki_ref[pl.ds(off, kt), :]
        sc = jnp.zeros((QB, kt), F32)
        for h in range(IDX_HEADS):
            d = _dot_nt(qi[:, h * IDX_DIM:(h + 1) * IDX_DIM], ki_t)
            sc = sc + wi[:, IDX_DIM + h:IDX_DIM + h + 1] * jnp.maximum(d, 0.0)
        bits = pltpu.bitcast(sc, I32)
        key = bits ^ ((bits >> 31) & jnp.int32(0x7FFFFFFF))
        key_ref[:, pl.ds(off, kt)] = jnp.where(off + lane < lim, key, imin)
        return carry

    lax.fori_loop(0, n_t, score_tile, 0)

    def count(pred):
        def body(t, acc):
            off = pl.multiple_of(t * kt, kt)
            hit = jnp.where(pred(key_ref[:, pl.ds(off, kt)], off), 1, 0)
            for j in range(kt // LANES):
                acc = acc + hit[:, j * LANES:(j + 1) * LANES]
            return acc

        acc = lax.fori_loop(0, n_t, body, jnp.zeros((QB, LANES), I32))
        return jnp.sum(acc, axis=1, keepdims=True)

    def bisect(i, lo):
        cand = lo + jnp.left_shift(jnp.int32(1), 31 - i)
        cnt = count(lambda key, off: key >= cand)
        return jnp.where(cnt >= ksel, cand, lo)

    tau = lax.fori_loop(0, 32, bisect, jnp.full((QB, 1), INT_MIN, I32))

    need = ksel - count(lambda key, off: key > tau)
    n_bits = int(np.log2(S))

    def tie_bisect(i, jlo):
        cand = jlo + jnp.left_shift(jnp.int32(1), n_bits - 1 - i)
        cnt = count(lambda key, off: jnp.where(key == tau, off + lane, S) < cand)
        return jnp.where(cnt < need, cand, jlo)

    jmax = lax.fori_loop(0, n_bits, tie_bisect, jnp.zeros((QB, 1), I32))

    m_ref[...] = jnp.full_like(m_ref, NEG_BIG)
    l_ref[...] = jnp.zeros_like(l_ref)
    acc_ref[...] = jnp.zeros_like(acc_ref)
    qn = qn_ref[...]

    def att_tile(t, carry):
        off = pl.multiple_of(t * kt, kt)
        key = key_ref[:, pl.ds(off, kt)]
        tie_ok = jnp.where(key == tau, off + lane, S) <= jmax
        bias = jnp.where(key > tau, 0.0, jnp.where(tie_ok, 0.0, NEG_BIG))
        bias = jnp.where(key == imin, NEG_BIG, bias)
        k_t = kn_ref[pl.ds(off, kt), :]
        v_t = v_ref[pl.ds(off, kt), :]
        for h in range(N_HEADS):
            hs = slice(h * HEAD_DIM, (h + 1) * HEAD_DIM)
            s = _dot_nt(qn[:, hs], k_t[:, hs]) + bias
            m_old = m_ref[h]
            m_new = jnp.maximum(m_old, jnp.max(s, axis=1, keepdims=True))
            p = jnp.exp(s - m_new)
            alpha = jnp.exp(m_old - m_new)
            l_ref[h] = alpha * l_ref[h] + jnp.sum(p, axis=1, keepdims=True)
            acc_ref[:, hs] = alpha * acc_ref[:, hs] + _dot(p, v_t[:, hs])
            m_ref[h] = m_new
        return carry

    lax.fori_loop(0, n_t, att_tile, 0)

    g = g_ref[...]
    gate = g * jax.nn.sigmoid(g)
    for h in range(N_HEADS):
        hs = slice(h * HEAD_DIM, (h + 1) * HEAD_DIM)
        o_ref[:, hs] = (acc_ref[:, hs] / l_ref[h] * gate[:, hs]).astype(o_ref.dtype)


def _dsa(u, qn, kn, vb, qi, ki, wi, B, S, ksel):
    T = u.shape[0]
    QB = min(128, S)
    kt = min(512, S)
    nq = S // QB
    g_blk = (OFF_BM + 3 * D_BRANCH) // D_BRANCH
    qrow = lambda n: pl.BlockSpec((QB, n), lambda b, i: (b * nq + i, 0))
    full = lambda n: pl.BlockSpec((S, n), lambda b, i: (b, 0))
    return pl.pallas_call(
        functools.partial(_dsa_body, ksel=ksel, kt=kt),
        grid=(B, nq),
        in_specs=[qrow(D_BRANCH), qrow(IDX_HEADS * IDX_DIM), qrow(LANES),
                  pl.BlockSpec((QB, D_BRANCH), lambda b, i: (b * nq + i, g_blk)),
                  full(D_BRANCH), full(D_BRANCH), full(IDX_DIM)],
        out_specs=qrow(D_BRANCH),
        out_shape=jax.ShapeDtypeStruct((T, D_BRANCH), BF16),
        scratch_shapes=[pltpu.VMEM((QB, S), I32),
                        pltpu.VMEM((N_HEADS, QB, 1), F32),
                        pltpu.VMEM((N_HEADS, QB, 1), F32),
                        pltpu.VMEM((QB, D_BRANCH), F32)],
        compiler_params=_params("parallel", "arbitrary"),
        name="dsa",
    )(qn, qi, wi, u, kn, vb, ki)


def _lru_body(u_ref, cw_ref, cb_ref, wri_ref, bri_ref, lam_ref, o_ref, xbuf_ref, h_ref):
    TS = u_ref.shape[0]
    DB = D_BRANCH
    PAD = 8

    @pl.when(pl.program_id(1) == 0)
    def _():
        xbuf_ref[0:PAD, :] = jnp.zeros((PAD, DB), F32)
        h_ref[...] = jnp.zeros_like(h_ref)

    x = u_ref[:, 0:DB]
    g = u_ref[:, DB:2 * DB]
    xbuf_ref[PAD:PAD + TS, :] = x
    xc = cb_ref[...]
    for j in range(CONV_WIDTH):
        xc = xc + cw_ref[j:j + 1, :] * xbuf_ref[pl.ds(PAD - (CONV_WIDTH - 1) + j, TS), :]
    xbuf_ref[0:PAD, :] = x[TS - PAD:TS, :]

    ri = jax.nn.sigmoid(_dot(xc, wri_ref[...]) + bri_ref[...])
    r, i = ri[:, 0:DB], ri[:, DB:2 * DB]
    log_a = (-LRU_C) * r * jax.nn.softplus(-lam_ref[...])
    a = jnp.exp(log_a)
    b = jnp.sqrt(jnp.tanh(-log_a) * (1.0 + a * a)) * (i * xc)

    row = lax.broadcasted_iota(I32, (TS, 1), 0)
    step = 1
    while step < TS:
        live = row >= step
        a_s = jnp.where(live, pltpu.roll(a, step, 0), 1.0)
        b_s = jnp.where(live, pltpu.roll(b, step, 0), 0.0)
        b = a * b_s + b
        a = a * a_s
        step *= 2
    h = b + a * h_ref[...]
    h_ref[...] = h[TS - 1:TS, :]
    o_ref[...] = (h * (g * jax.nn.sigmoid(g))).astype(o_ref.dtype)


def _lru(u, B, S, conv_w, conv_b, w_ri, b_ri, lam):
    T = u.shape[0]
    TS = min(256, S)
    ns = S // TS
    vec = lambda n: pl.BlockSpec((1, n), lambda b, s: (0, 0))
    return pl.pallas_call(
        _lru_body,
        grid=(B, ns),
        in_specs=[pl.BlockSpec((TS, W_C), lambda b, s: (b * ns + s, OFF_C // W_C)),
                  pl.BlockSpec((CONV_WIDTH, D_BRANCH), lambda b, s: (0, 0)),
                  vec(D_BRANCH),
                  pl.BlockSpec((D_BRANCH, 2 * D_BRANCH), lambda b, s: (0, 0)),
                  vec(2 * D_BRANCH), vec(D_BRANCH)],
        out_specs=pl.BlockSpec((TS, D_BRANCH), lambda b, s: (b * ns + s, 0)),
        out_shape=jax.ShapeDtypeStruct((T, D_BRANCH), BF16),
        scratch_shapes=[pltpu.VMEM((TS + 8, D_BRANCH), F32), pltpu.VMEM((1, D_BRANCH), F32)],
        compiler_params=_params("parallel", "arbitrary"),
        name="lru",
    )(u, conv_w, conv_b, w_ri, b_ri, lam)


def _merge_body(ya_ref, yb_ref, yc_ref, ug_ref, h_ref, p_ref, wb_ref, wo_ref, wp_ref, wpg_ref,
                o_ref):
    D = D_MODEL
    merged = None
    for n, y_ref in enumerate((ya_ref, yb_ref, yc_ref)):
        term = jax.nn.sigmoid(ug_ref[:, n * D:(n + 1) * D]) * _dot(y_ref[...], wb_ref[n])
        merged = term if merged is None else merged + term
    h1 = h_ref[...] + _dot(merged, wo_ref[...])
    gate = jax.nn.sigmoid(_dot(h1, wpg_ref[...]))
    o_ref[...] = h1 + gate * _dot(p_ref[...], wp_ref[...])


def _merge(ya, yb, yc, u, h, p, wb, wo, wp, wpg):
    T, D = h.shape
    tm = min(256, T)
    row = lambda n: pl.BlockSpec((tm, n), lambda i: (i, 0))
    return pl.pallas_call(
        _merge_body,
        grid=(T // tm,),
        in_specs=[row(D_BRANCH), row(D_BRANCH), row(D_BRANCH),
                  pl.BlockSpec((tm, W_G), lambda i: (i, OFF_G // W_G)),
                  row(D), row(PLE_DIM),
                  pl.BlockSpec((N_BRANCH, D_BRANCH, D), lambda i: (0, 0, 0)),
                  pl.BlockSpec((D, D), lambda i: (0, 0)),
                  pl.BlockSpec((PLE_DIM, D), lambda i: (0, 0)),
                  pl.BlockSpec((D, D), lambda i: (0, 0))],
        out_specs=row(D),
        out_shape=jax.ShapeDtypeStruct((T, D), F32),
        compiler_params=_params("parallel"),
        name="merge",
    )(ya, yb, yc, u, h, p, wb, wo, wp, wpg)


def _reorder_w_in(w):
    a0 = 0
    b0 = a0 + W_A
    c0 = b0 + 3 * D_BRANCH + IDX_HEADS * IDX_DIM + IDX_DIM + IDX_HEADS + D_BRANCH
    g0 = c0 + W_C
    bq = w[:, b0:b0 + 3 * D_BRANCH]
    bi0 = b0 + 3 * D_BRANCH
    n_idx = IDX_HEADS * IDX_DIM + IDX_DIM + IDX_HEADS
    bidx = w[:, bi0:bi0 + n_idx]
    bg = w[:, bi0 + n_idx:c0]
    pad = jnp.zeros((w.shape[0], W_BI - n_idx), w.dtype)
    out = jnp.concatenate([w[:, g0:g0 + W_G], w[:, c0:g0], bq, bg, bidx, pad, w[:, a0:b0]], axis=1)
    return out.astype(BF16)


def _block_diag(w):
    H, n, _ = w.shape
    eye = jnp.eye(H, dtype=w.dtype)
    return (eye[:, None, :, None] * w[:, :, None, :]).reshape(H * n, H * n)


def _rope_tables(positions):
    half = ROPE_DIM // 2
    inv_freq = ROPE_THETA ** (-jnp.arange(0, ROPE_DIM, 2, dtype=F32) / ROPE_DIM)
    ang = positions.astype(F32).reshape(-1, 1) * inv_freq
    cos, sin = jnp.cos(ang), jnp.sin(ang)
    T = cos.shape[0]
    rest = HEAD_DIM - ROPE_DIM
    cos_h = jnp.concatenate([cos, cos, jnp.ones((T, rest), F32)], axis=1)
    sin_h = jnp.concatenate([-sin, sin, jnp.zeros((T, rest), F32)], axis=1)
    return jnp.tile(cos_h, (1, LANES // HEAD_DIM)), jnp.tile(sin_h, (1, LANES // HEAD_DIM))


def kernel(x, p, positions, norm_g, w_in, rwkv_mu, rwkv_w0, rwkv_w2, rwkv_a0, rwkv_a2, rwkv_k_k,
           rwkv_k_a, rwkv_r_k, rwkv_gn_g, rwkv_gn_b, dsa_q_g, dsa_k_g, lru_conv_w, lru_conv_b,
           lru_w_r, lru_b_r, lru_w_i, lru_b_i, lru_lambda, w_branch, w_out, w_ple, w_ple_gate):
    B, S, D = x.shape
    depth = w_in.shape[0]
    T = B * S
    ksel = min(TOPK_MAX, S // 4)
    cosf, sinf = _rope_tables(positions)
    head_id = np.arange(D_BRANCH) // HEAD_DIM
    hsel = jnp.asarray(head_id[:, None] == head_id[None, :], BF16)
    row = lambda v: v.reshape(1, -1)
    per_head = lambda v: jnp.tile(v, N_HEADS).reshape(1, -1)

    h = x.reshape(T, D)
    for i in range(depth):
        u = _inproj(h, row(norm_g[i]), _reorder_w_in(w_in[i]))
        ya = _rwkv(u, B, S, row(rwkv_mu[i]), row(rwkv_w0[i]), rwkv_w2[i], row(rwkv_a0[i]),
                   rwkv_a2[i], row(rwkv_k_k[i]), row(rwkv_k_a[i]), row(rwkv_r_k[i]),
                   row(rwkv_gn_g[i]), row(rwkv_gn_b[i]), hsel)
        qn, kn, vb, qi, ki, wi = _dsa_prep(u, cosf, sinf, per_head(dsa_q_g[i]),
                                           per_head(dsa_k_g[i]), hsel)
        yb = _dsa(u, qn, kn, vb, qi, ki, wi, B, S, ksel)
        w_ri = jnp.concatenate([_block_diag(lru_w_r[i]), _block_diag(lru_w_i[i])], axis=1)
        b_ri = jnp.concatenate([lru_b_r[i], lru_b_i[i]]).reshape(1, -1)
        yc = _lru(u, B, S, lru_conv_w[i], row(lru_conv_b[i]), w_ri.astype(BF16), b_ri,
                  row(lru_lambda[i]))
        h = _merge(ya, yb, yc, u, h, p[i].reshape(T, PLE_DIM), w_branch[i].astype(BF16),
                   w_out[i].astype(BF16), w_ple[i].astype(BF16), w_ple_gate[i].astype(BF16))
    return h.reshape(B, S, D)
```

```python
import functools

import numpy as np
import jax
import jax.numpy as jnp
from jax import lax
from jax.experimental import pallas as pl
from jax.experimental.pallas import tpu as pltpu

F32 = jnp.float32
BF16 = jnp.bfloat16
I32 = jnp.int32

D_MODEL = 1024
D_BRANCH = 512
HEAD_DIM = 64
N_HEADS = 8
N_BRANCH = 3
PLE_DIM = 256
CHUNK = 64
CHUNK_SHIFT = 6
NORM_EPS = 1e-6
LORA = 64
RWKV_GN_EPS = 64e-5
IDX_HEADS = 4
IDX_DIM = 64
TOPK_MAX = 256
ROPE_THETA = 500000.0
ROPE_DIM = 16
CONV_WIDTH = 4
LRU_C = 8.0

LANES = 128
VMEM_LIMIT = 48 * 1024 * 1024

W_G = N_BRANCH * D_MODEL
W_C = 2 * D_BRANCH
W_BM = 4 * D_BRANCH
W_BI = 384
W_A = 4 * D_BRANCH + 2 * LORA
OFF_G, OFF_C, OFF_BM, OFF_BI, OFF_A = 0, 3072, 4096, 6144, 6528
D_IN_PAD = OFF_A + W_A

NEG_BIG = -1e30
INT_MIN = -2 ** 31


def _dot(a, b):
    return jnp.dot(a.astype(BF16), b.astype(BF16), preferred_element_type=F32)


def _dot_nt(a, b):
    return lax.dot_general(a.astype(BF16), b.astype(BF16), (((1,), (1,)), ((), ())),
                           preferred_element_type=F32)


SMALL_DOT_PIECES = 1


def _pieces(x):
    hi = x.astype(BF16)
    if SMALL_DOT_PIECES == 1:
        return [hi]
    return [hi, (x - hi.astype(F32)).astype(BF16)]


def _sdot_general(a, b, dims):
    pa, pb = _pieces(a), _pieces(b)
    out = None
    for i, x in enumerate(pa):
        for j, y in enumerate(pb):
            if i + j < SMALL_DOT_PIECES:
                t = lax.dot_general(x, y, (dims, ((), ())), preferred_element_type=F32)
                out = t if out is None else out + t
    return out


def _sdot(a, b):
    return _sdot_general(a, b, ((1,), (0,)))


def _sdot_nt(a, b):
    return _sdot_general(a, b, ((1,), (1,)))


def _dot_sel(x, sel_bf16):
    hi = x.astype(BF16)
    r1 = x - hi.astype(F32)
    mid = r1.astype(BF16)
    lo = (r1 - mid.astype(F32)).astype(BF16)
    out = jnp.dot(hi, sel_bf16, preferred_element_type=F32)
    out = out + jnp.dot(mid, sel_bf16, preferred_element_type=F32)
    return out + jnp.dot(lo, sel_bf16, preferred_element_type=F32)


def _params(*sem):
    return pltpu.CompilerParams(dimension_semantics=sem, vmem_limit_bytes=VMEM_LIMIT)


def _inproj_body(h_ref, g_ref, w_ref, o_ref, hn_ref):
    @pl.when(pl.program_id(1) == 0)
    def _():
        x = h_ref[...]
        ms = jnp.mean(x * x, axis=-1, keepdims=True)
        hn_ref[...] = (x * lax.rsqrt(ms + NORM_EPS) * g_ref[...]).astype(BF16)

    o_ref[...] = jnp.dot(hn_ref[...], w_ref[...], preferred_element_type=F32)


def _inproj(h, g, w):
    T, D = h.shape
    N = w.shape[1]
    tm, tn = min(1024, T), 512
    return pl.pallas_call(
        _inproj_body,
        grid=(T // tm, N // tn),
        in_specs=[pl.BlockSpec((tm, D), lambda i, j: (i, 0)),
                  pl.BlockSpec((1, D), lambda i, j: (0, 0)),
                  pl.BlockSpec((D, tn), lambda i, j: (0, j))],
        out_specs=pl.BlockSpec((tm, tn), lambda i, j: (i, j)),
        out_shape=jax.ShapeDtypeStruct((T, N), F32),
        scratch_shapes=[pltpu.VMEM((tm, D), BF16)],
        compiler_params=_params("parallel", "arbitrary"),
        name="inproj",
    )(h, g, w)


def _rwkv_body(ua_ref, mu_ref, w0_ref, w2_ref, a0_ref, a2_ref, kk_ref, ka_ref, rk_ref,
               gng_ref, gnb_ref, hsel_ref, o_ref, z_ref, prev_ref, obuf_ref):
    C = ua_ref.shape[0]
    DB = D_BRANCH

    @pl.when(pl.program_id(1) == 0)
    def _():
        z_ref[...] = jnp.zeros_like(z_ref)
        prev_ref[...] = jnp.zeros_like(prev_ref)

    ua = ua_ref[...]
    row = lax.broadcasted_iota(I32, (C, 1), 0)
    shifted = jnp.where(row == 0, prev_ref[...], pltpu.roll(ua, 1, 0))
    prev_ref[...] = ua[C - 1:C, :]
    xs = ua + (shifted - ua) * mu_ref[...]

    r = xs[:, 0:DB]
    k = xs[:, DB:2 * DB]
    v = xs[:, 2 * DB:3 * DB]
    wl = xs[:, 3 * DB:3 * DB + LORA]
    al = xs[:, 3 * DB + LORA:3 * DB + 2 * LORA]
    g = xs[:, 3 * DB + 2 * LORA:]
    hsel = hsel_ref[...]

    w_pre = w0_ref[...] + _dot(jnp.tanh(wl), w2_ref[...])
    lw = -jnp.exp(-jax.nn.softplus(-w_pre) - 0.5)
    a = jax.nn.sigmoid(a0_ref[...] + _dot(al, a2_ref[...]))
    kk = k * kk_ref[...]
    kk = kk / jnp.maximum(jnp.sqrt(_dot_sel(kk * kk, hsel)), 1e-12)
    k_mod = k * (1.0 + (a - 1.0) * ka_ref[...])
    b = kk * a

    ci = lax.broadcasted_iota(I32, (C, C), 0)
    cj = lax.broadcasted_iota(I32, (C, C), 1)
    incl = ci >= cj
    strict = ci > cj
    eye = ci == cj
    cl = _dot_sel_lhs(jnp.where(incl, 1.0, 0.0).astype(BF16), lw)
    cl_end = cl[C - 1:C, :]
    g_incl = jnp.exp(cl)
    g_inv = jnp.exp(-cl)
    d_end = jnp.exp(cl_end - cl)
    kt_all = kk * jnp.exp(cl - lw)
    rt_all = r * g_incl
    bs_all = b * g_inv
    ks_all = k_mod * g_inv
    bh_all = b * d_end
    kh_all = k_mod * d_end
    g_end = g_incl[C - 1:C, :]

    bh_t = bh_all.T
    kh_t = kh_all.T
    ident = jnp.where(eye, 1.0, 0.0)
    n_sq = int(np.log2(C)) - 1
    heads = range(N_HEADS)
    hsl = [slice(h * HEAD_DIM, (h + 1) * HEAD_DIM) for h in heads]
    kt = [kt_all[:, s] for s in hsl]
    rt = [rt_all[:, s] for s in hsl]
    vh = [v[:, s] for s in hsl]
    gram = [_sdot_nt(jnp.concatenate([kt[h], rt[h]], axis=0),
                     jnp.concatenate([bs_all[:, hsl[h]], ks_all[:, hsl[h]]], axis=0)) for h in heads]
    b_m = [jnp.where(strict, gram[h][0:C, C:2 * C], 0.0) for h in heads]
    ab = [jnp.where(incl, gram[h][C:2 * C, 0:C], 0.0) for h in heads]
    bk = [jnp.where(incl, gram[h][C:2 * C, C:2 * C], 0.0) for h in heads]
    pw = [jnp.where(strict, -gram[h][0:C, 0:C], 0.0) for h in heads]
    tinv = [ident + pw[h] for h in heads]
    bmv = [_sdot(b_m[h], vh[h]) for h in heads]
    nv = [_sdot(jnp.concatenate([kh_t[hsl[h], :], bk[h]], axis=0), vh[h]) for h in heads]
    for _ in range(n_sq):
        pw = [_sdot(pw[h], pw[h]) for h in heads]
        tinv = [_sdot(tinv[h], ident + pw[h]) for h in heads]
    wu = [-_sdot(tinv[h], jnp.concatenate([kt[h], bmv[h]], axis=1)) for h in heads]
    mq = [_sdot(jnp.concatenate([bh_t[hsl[h], :], ab[h]], axis=0), wu[h]) for h in heads]
    lhs = []
    for h in heads:
        m_m = jnp.where(eye, g_end[:, hsl[h]], 0.0) + mq[h][0:HEAD_DIM, 0:HEAD_DIM]
        qe = rt[h] + mq[h][HEAD_DIM:, 0:HEAD_DIM]
        lhs.append(jnp.concatenate([qe, m_m], axis=0))
    oz = [_sdot(lhs[h], z_ref[h]) for h in heads]
    for h in heads:
        n_m = mq[h][0:HEAD_DIM, HEAD_DIM:] + nv[h][0:HEAD_DIM, :]
        o0 = mq[h][HEAD_DIM:, HEAD_DIM:] + nv[h][HEAD_DIM:, :]
        obuf_ref[:, hsl[h]] = oz[h][0:C, :] + o0
        z_ref[h] = oz[h][C:, :] + n_m

    o = obuf_ref[...]
    inv_n = 1.0 / HEAD_DIM
    mean = _dot_sel(o, hsel) * inv_n
    oc = o - mean
    var = _dot_sel(oc * oc, hsel) * inv_n
    o = oc * lax.rsqrt(var + RWKV_GN_EPS) * gng_ref[...] + gnb_ref[...]
    bonus = _dot_sel(r * k_mod * rk_ref[...], hsel) * v
    o_ref[...] = ((o + bonus) * (g * jax.nn.sigmoid(g))).astype(o_ref.dtype)


def _dot_sel_lhs(sel_bf16, x):
    hi = x.astype(BF16)
    r1 = x - hi.astype(F32)
    mid = r1.astype(BF16)
    lo = (r1 - mid.astype(F32)).astype(BF16)
    out = jnp.dot(sel_bf16, hi, preferred_element_type=F32)
    out = out + jnp.dot(sel_bf16, mid, preferred_element_type=F32)
    return out + jnp.dot(sel_bf16, lo, preferred_element_type=F32)


def _rwkv(u, B, S, mu, w0, w2, a0, a2, k_k, k_a, r_k, gn_g, gn_b, hsel):
    T = u.shape[0]
    C = CHUNK
    nc = S // C
    a_blk = OFF_A // W_A
    vec = lambda n: pl.BlockSpec((1, n), lambda b, c: (0, 0))
    mat = lambda m, n: pl.BlockSpec((m, n), lambda b, c: (0, 0))
    return pl.pallas_call(
        _rwkv_body,
        grid=(B, nc),
        in_specs=[pl.BlockSpec((C, W_A), lambda b, c: (b * nc + c, a_blk)),
                  vec(W_A), vec(D_BRANCH), mat(LORA, D_BRANCH), vec(D_BRANCH), mat(LORA, D_BRANCH),
                  vec(D_BRANCH), vec(D_BRANCH), vec(D_BRANCH), vec(D_BRANCH), vec(D_BRANCH),
                  mat(D_BRANCH, D_BRANCH)],
        out_specs=pl.BlockSpec((C, D_BRANCH), lambda b, c: (b * nc + c, 0)),
        out_shape=jax.ShapeDtypeStruct((T, D_BRANCH), BF16),
        scratch_shapes=[pltpu.VMEM((N_HEADS, HEAD_DIM, HEAD_DIM), F32),
                        pltpu.VMEM((1, W_A), F32),
                        pltpu.VMEM((C, D_BRANCH), F32)],
        compiler_params=_params("parallel", "arbitrary"),
        name="rwkv",
    )(u, mu, w0, w2, a0, a2, k_k, k_a, r_k, gn_g, gn_b, hsel)


def _rope(x, cosw, sinw):
    W = x.shape[1]
    lane = lax.broadcasted_iota(I32, (1, W), 1) & (HEAD_DIM - 1)
    half = ROPE_DIM // 2
    partner = jnp.where(lane < half, pltpu.roll(x, W - half, 1), pltpu.roll(x, half, 1))
    return x * cosw + partner * sinw


def _dsa_prep_body(um_ref, ui_ref, cos_ref, sin_ref, qg_ref, kg_ref, hsel_ref,
                   qn_ref, kn_ref, vb_ref, qi_ref, ki_ref, wi_ref):
    DB = D_BRANCH
    cos1, sin1 = cos_ref[...], sin_ref[...]
    cos2, sin2 = jnp.concatenate([cos1] * 2, axis=1), jnp.concatenate([sin1] * 2, axis=1)
    cos4, sin4 = jnp.concatenate([cos2] * 2, axis=1), jnp.concatenate([sin2] * 2, axis=1)
    hsel = hsel_ref[...]

    def headnorm(x, gain):
        ms = _dot_sel(x * x, hsel) * (1.0 / HEAD_DIM)
        return x * lax.rsqrt(ms + NORM_EPS) * gain

    q = _rope(headnorm(um_ref[:, 0:DB], qg_ref[...]), cos4, sin4)
    k = _rope(headnorm(um_ref[:, DB:2 * DB], kg_ref[...]), cos4, sin4)
    qn_ref[...] = (q * (HEAD_DIM ** -0.5)).astype(BF16)
    kn_ref[...] = k.astype(BF16)
    vb_ref[...] = um_ref[:, 2 * DB:3 * DB].astype(BF16)
    qi_ref[...] = _rope(ui_ref[:, 0:IDX_HEADS * IDX_DIM], cos2, sin2).astype(BF16)
    kw = ui_ref[:, IDX_HEADS * IDX_DIM:]
    ki_ref[...] = _rope(kw, cos1, sin1)[:, 0:IDX_DIM].astype(BF16)
    wi_ref[...] = kw * (IDX_HEADS ** -0.5 * IDX_DIM ** -0.5)


def _dsa_prep(u, cosf, sinf, qg, kg, hsel):
    T = u.shape[0]
    tm = min(512, T)
    vec = lambda n: pl.BlockSpec((1, n), lambda i: (0, 0))
    row = lambda n: pl.BlockSpec((tm, n), lambda i: (i, 0))
    return pl.pallas_call(
        _dsa_prep_body,
        grid=(T // tm,),
        in_specs=[pl.BlockSpec((tm, W_BM), lambda i: (i, OFF_BM // W_BM)),
                  pl.BlockSpec((tm, W_BI), lambda i: (i, OFF_BI // W_BI)),
                  row(LANES), row(LANES), vec(D_BRANCH), vec(D_BRANCH),
                  pl.BlockSpec((D_BRANCH, D_BRANCH), lambda i: (0, 0))],
        out_specs=[row(D_BRANCH), row(D_BRANCH), row(D_BRANCH), row(IDX_HEADS * IDX_DIM),
                   row(IDX_DIM), row(LANES)],
        out_shape=[jax.ShapeDtypeStruct((T, D_BRANCH), BF16)] * 3
        + [jax.ShapeDtypeStruct((T, IDX_HEADS * IDX_DIM), BF16),
           jax.ShapeDtypeStruct((T, IDX_DIM), BF16),
           jax.ShapeDtypeStruct((T, LANES), F32)],
        compiler_params=_params("parallel"),
        name="dsa_prep",
    )(u, u, cosf, sinf, qg, kg, hsel)


def _dsa_body(qn_ref, qi_ref, wi_ref, g_ref, kn_ref, v_ref, ki_ref, o_ref,
              key_ref, m_ref, l_ref, acc_ref, *, ksel, kt):
    QB = qn_ref.shape[0]
    S = kn_ref.shape[0]
    q0 = pl.program_id(1) * QB
    row = lax.broadcasted_iota(I32, (QB, 1), 0)
    lim = (((q0 + row) >> CHUNK_SHIFT) + 1) << CHUNK_SHIFT
    n_t = (q0 + QB + kt - 1) // kt
    lane = lax.broadcasted_iota(I32, (1, kt), 1)
    imin = jnp.int32(INT_MIN)
    qi = qi_ref[...]
    wi = wi_ref[...]

    def score_tile(t, carry):
        off = pl.multiple_of(t * kt, kt)
        ki_t = ki_ref[pl.ds(off, kt), :]
        sc = jnp.zeros((QB, kt), F32)
        for h in range(IDX_HEADS):
            d = _dot_nt(qi[:, h * IDX_DIM:(h + 1) * IDX_DIM], ki_t)
            sc = sc + wi[:, IDX_DIM + h:IDX_DIM + h + 1] * jnp.maximum(d, 0.0)
        bits = pltpu.bitcast(sc, I32)
        key = bits ^ ((bits >> 31) & jnp.int32(0x7FFFFFFF))
        key_ref[:, pl.ds(off, kt)] = jnp.where(off + lane < lim, key, imin)
        return carry

    lax.fori_loop(0, n_t, score_tile, 0)

    def count(pred):
        def body(t, acc):
            off = pl.multiple_of(t * kt, kt)
            hit = jnp.where(pred(key_ref[:, pl.ds(off, kt)], off), 1, 0)
            for j in range(kt // LANES):
                acc = acc + hit[:, j * LANES:(j + 1) * LANES]
            return acc

        acc = lax.fori_loop(0, n_t, body, jnp.zeros((QB, LANES), I32))
        return jnp.sum(acc, axis=1, keepdims=True)

    def bisect(i, lo):
        cand = lo + jnp.left_shift(jnp.int32(1), 31 - i)
        cnt = count(lambda key, off: key >= cand)
        return jnp.where(cnt >= ksel, cand, lo)

    tau = lax.fori_loop(0, 32, bisect, jnp.full((QB, 1), INT_MIN, I32))

    need = ksel - count(lambda key, off: key > tau)
    n_bits = int(np.log2(S))

    def tie_bisect(i, jlo):
        cand = jlo + jnp.left_shift(jnp.int32(1), n_bits - 1 - i)
        cnt = count(lambda key, off: jnp.where(key == tau, off + lane, S) < cand)
        return jnp.where(cnt < need, cand, jlo)

    has_ties = jnp.max(count(lambda key, off: key >= tau)) > ksel
    jmax = lax.cond(has_ties,
                    lambda: lax.fori_loop(0, n_bits, tie_bisect, jnp.zeros((QB, 1), I32)),
                    lambda: jnp.full((QB, 1), S, I32))

    m_ref[...] = jnp.full_like(m_ref, NEG_BIG)
    l_ref[...] = jnp.zeros_like(l_ref)
    acc_ref[...] = jnp.zeros_like(acc_ref)
    low_half = lax.broadcasted_iota(I32, (1, LANES), 1) < HEAD_DIM
    qn = qn_ref[...].astype(F32)
    q_pair = []
    for h in range(N_HEADS):
        pair = qn[:, (h // 2) * LANES:(h // 2 + 1) * LANES]
        own = low_half if h % 2 == 0 else jnp.logical_not(low_half)
        q_pair.append(jnp.where(own, pair, 0.0).astype(BF16))
    ones = jnp.ones((kt, LANES), BF16)

    def att_tile(t, carry):
        off = pl.multiple_of(t * kt, kt)
        key = key_ref[:, pl.ds(off, kt)]
        tie_ok = jnp.where(key == tau, off + lane, S) <= jmax
        bias = jnp.where(key > tau, 0.0, jnp.where(tie_ok, 0.0, NEG_BIG))
        bias = jnp.where(key == imin, NEG_BIG, bias)
        heads = range(N_HEADS)
        pairs = [slice(pr * LANES, (pr + 1) * LANES) for pr in range(N_HEADS // 2)]
        k_p = [kn_ref[pl.ds(off, kt), ps] for ps in pairs]
        vo_p = [jnp.concatenate([v_ref[pl.ds(off, kt), ps], ones], axis=1) for ps in pairs]
        s = [lax.dot_general(q_pair[h], k_p[h // 2], (((1,), (1,)), ((), ())),
                             preferred_element_type=F32) + bias for h in heads]
        m_old = [m_ref[h] for h in heads]
        m_new = [jnp.maximum(m_old[h], jnp.max(s[h], axis=1, keepdims=True)) for h in heads]
        p = [jnp.exp(s[h] - jnp.concatenate([m_new[h]] * (kt // LANES), axis=1)).astype(BF16)
             for h in heads]
        alpha = [jnp.exp(m_old[h] - m_new[h]) for h in heads]
        pvl = [jnp.dot(p[h], vo_p[h // 2], preferred_element_type=F32) for h in heads]
        for h in heads:
            l_ref[h] = alpha[h] * l_ref[h] + pvl[h][:, LANES:]
            m_ref[h] = m_new[h]
        for pr, ps in enumerate(pairs):
            alpha_p = jnp.where(low_half, alpha[2 * pr], alpha[2 * pr + 1])
            pv = jnp.where(low_half, pvl[2 * pr][:, 0:LANES], pvl[2 * pr + 1][:, 0:LANES])
            acc_ref[:, ps] = alpha_p * acc_ref[:, ps] + pv
        return carry

    lax.fori_loop(0, n_t, att_tile, 0)

    g = g_ref[...]
    gate = g * jax.nn.sigmoid(g)
    for pr in range(N_HEADS // 2):
        ps = slice(pr * LANES, (pr + 1) * LANES)
        l_p = jnp.where(low_half, l_ref[2 * pr], l_ref[2 * pr + 1])
        o_ref[:, ps] = (acc_ref[:, ps] / l_p * gate[:, ps]).astype(o_ref.dtype)


def _dsa(u, qn, kn, vb, qi, ki, wi, B, S, ksel):
    T = u.shape[0]
    QB = min(128, S)
    kt = min(512, S)
    nq = S // QB
    g_blk = (OFF_BM + 3 * D_BRANCH) // D_BRANCH
    qrow = lambda n: pl.BlockSpec((QB, n), lambda b, i: (b * nq + i, 0))
    full = lambda n: pl.BlockSpec((S, n), lambda b, i: (b, 0))
    return pl.pallas_call(
        functools.partial(_dsa_body, ksel=ksel, kt=kt),
        grid=(B, nq),
        in_specs=[qrow(D_BRANCH), qrow(IDX_HEADS * IDX_DIM), qrow(LANES),
                  pl.BlockSpec((QB, D_BRANCH), lambda b, i: (b * nq + i, g_blk)),
                  full(D_BRANCH), full(D_BRANCH), full(IDX_DIM)],
        out_specs=qrow(D_BRANCH),
        out_shape=jax.ShapeDtypeStruct((T, D_BRANCH), BF16),
        scratch_shapes=[pltpu.VMEM((QB, S), I32),
                        pltpu.VMEM((N_HEADS, QB, LANES), F32),
                        pltpu.VMEM((N_HEADS, QB, LANES), F32),
                        pltpu.VMEM((QB, D_BRANCH), F32)],
        compiler_params=_params("parallel", "arbitrary"),
        name="dsa",
    )(qn, qi, wi, u, kn, vb, ki)


def _lru_body(u_ref, cw_ref, cb_ref, wri_ref, bri_ref, lam_ref, o_ref, xbuf_ref, h_ref):
    TS = u_ref.shape[0]
    DB = D_BRANCH
    PAD = 8

    @pl.when(pl.program_id(1) == 0)
    def _():
        xbuf_ref[0:PAD, :] = jnp.zeros((PAD, DB), F32)
        h_ref[...] = jnp.zeros_like(h_ref)

    x = u_ref[:, 0:DB]
    g = u_ref[:, DB:2 * DB]
    xbuf_ref[PAD:PAD + TS, :] = x
    xc = cb_ref[...]
    for j in range(CONV_WIDTH):
        xc = xc + cw_ref[j:j + 1, :] * xbuf_ref[pl.ds(PAD - (CONV_WIDTH - 1) + j, TS), :]
    xbuf_ref[0:PAD, :] = x[TS - PAD:TS, :]

    ri = jax.nn.sigmoid(_dot(xc, wri_ref[...]) + bri_ref[...])
    r, i = ri[:, 0:DB], ri[:, DB:2 * DB]
    log_a = (-LRU_C) * r * jax.nn.softplus(-lam_ref[...])
    a = jnp.exp(log_a)
    b = jnp.sqrt(jnp.tanh(-log_a) * (1.0 + a * a)) * (i * xc)

    row = lax.broadcasted_iota(I32, (TS, 1), 0)
    step = 1
    while step < TS:
        live = row >= step
        a_s = jnp.where(live, pltpu.roll(a, step, 0), 1.0)
        b_s = jnp.where(live, pltpu.roll(b, step, 0), 0.0)
        b = a * b_s + b
        a = a * a_s
        step *= 2
    h = b + a * h_ref[...]
    h_ref[...] = h[TS - 1:TS, :]
    o_ref[...] = (h * (g * jax.nn.sigmoid(g))).astype(o_ref.dtype)


def _lru(u, B, S, conv_w, conv_b, w_ri, b_ri, lam):
    T = u.shape[0]
    TS = min(256, S)
    ns = S // TS
    vec = lambda n: pl.BlockSpec((1, n), lambda b, s: (0, 0))
    return pl.pallas_call(
        _lru_body,
        grid=(B, ns),
        in_specs=[pl.BlockSpec((TS, W_C), lambda b, s: (b * ns + s, OFF_C // W_C)),
                  pl.BlockSpec((CONV_WIDTH, D_BRANCH), lambda b, s: (0, 0)),
                  vec(D_BRANCH),
                  pl.BlockSpec((D_BRANCH, 2 * D_BRANCH), lambda b, s: (0, 0)),
                  vec(2 * D_BRANCH), vec(D_BRANCH)],
        out_specs=pl.BlockSpec((TS, D_BRANCH), lambda b, s: (b * ns + s, 0)),
        out_shape=jax.ShapeDtypeStruct((T, D_BRANCH), BF16),
        scratch_shapes=[pltpu.VMEM((TS + 8, D_BRANCH), F32), pltpu.VMEM((1, D_BRANCH), F32)],
        compiler_params=_params("parallel", "arbitrary"),
        name="lru",
    )(u, conv_w, conv_b, w_ri, b_ri, lam)


def _merge_body(ya_ref, yb_ref, yc_ref, ug_ref, h_ref, p_ref, wb_ref, wo_ref, wp_ref, wpg_ref,
                o_ref):
    D = D_MODEL
    merged = None
    for n, y_ref in enumerate((ya_ref, yb_ref, yc_ref)):
        term = jax.nn.sigmoid(ug_ref[:, n * D:(n + 1) * D]) * _dot(y_ref[...], wb_ref[n])
        merged = term if merged is None else merged + term
    h1 = h_ref[...] + _dot(merged, wo_ref[...])
    gate = jax.nn.sigmoid(_dot(h1, wpg_ref[...]))
    o_ref[...] = h1 + gate * _dot(p_ref[...], wp_ref[...])


def _merge(ya, yb, yc, u, h, p, wb, wo, wp, wpg):
    T, D = h.shape
    tm = min(256, T)
    row = lambda n: pl.BlockSpec((tm, n), lambda i: (i, 0))
    return pl.pallas_call(
        _merge_body,
        grid=(T // tm,),
        in_specs=[row(D_BRANCH), row(D_BRANCH), row(D_BRANCH),
                  pl.BlockSpec((tm, W_G), lambda i: (i, OFF_G // W_G)),
                  row(D), row(PLE_DIM),
                  pl.BlockSpec((N_BRANCH, D_BRANCH, D), lambda i: (0, 0, 0)),
                  pl.BlockSpec((D, D), lambda i: (0, 0)),
                  pl.BlockSpec((PLE_DIM, D), lambda i: (0, 0)),
                  pl.BlockSpec((D, D), lambda i: (0, 0))],
        out_specs=row(D),
        out_shape=jax.ShapeDtypeStruct((T, D), F32),
        compiler_params=_params("parallel"),
        name="merge",
    )(ya, yb, yc, u, h, p, wb, wo, wp, wpg)


def _reorder_w_in(w):
    a0 = 0
    b0 = a0 + W_A
    c0 = b0 + 3 * D_BRANCH + IDX_HEADS * IDX_DIM + IDX_DIM + IDX_HEADS + D_BRANCH
    g0 = c0 + W_C
    bq = w[:, b0:b0 + 3 * D_BRANCH]
    bi0 = b0 + 3 * D_BRANCH
    n_idx = IDX_HEADS * IDX_DIM + IDX_DIM + IDX_HEADS
    bidx = w[:, bi0:bi0 + n_idx]
    bg = w[:, bi0 + n_idx:c0]
    pad = jnp.zeros((w.shape[0], W_BI - n_idx), w.dtype)
    out = jnp.concatenate([w[:, g0:g0 + W_G], w[:, c0:g0], bq, bg, bidx, pad, w[:, a0:b0]], axis=1)
    return _to_bf16(out)


def _cast_body(x_ref, o_ref):
    o_ref[...] = x_ref[...].astype(o_ref.dtype)


def _to_bf16(w):
    R, N = w.shape
    tn = 512
    return pl.pallas_call(
        _cast_body,
        grid=(N // tn,),
        in_specs=[pl.BlockSpec((R, tn), lambda j: (0, j))],
        out_specs=pl.BlockSpec((R, tn), lambda j: (0, j)),
        out_shape=jax.ShapeDtypeStruct((R, N), BF16),
        compiler_params=_params("parallel"),
        name="cast_bf16",
    )(w)


def _block_diag(w):
    H, n, _ = w.shape
    eye = jnp.eye(H, dtype=w.dtype)
    return (eye[:, None, :, None] * w[:, :, None, :]).reshape(H * n, H * n)


def _rope_tables(positions):
    half = ROPE_DIM // 2
    inv_freq = ROPE_THETA ** (-jnp.arange(0, ROPE_DIM, 2, dtype=F32) / ROPE_DIM)
    ang = positions.astype(F32).reshape(-1, 1) * inv_freq
    cos, sin = jnp.cos(ang), jnp.sin(ang)
    T = cos.shape[0]
    rest = HEAD_DIM - ROPE_DIM
    cos_h = jnp.concatenate([cos, cos, jnp.ones((T, rest), F32)], axis=1)
    sin_h = jnp.concatenate([-sin, sin, jnp.zeros((T, rest), F32)], axis=1)
    return jnp.tile(cos_h, (1, LANES // HEAD_DIM)), jnp.tile(sin_h, (1, LANES // HEAD_DIM))


def kernel(x, p, positions, norm_g, w_in, rwkv_mu, rwkv_w0, rwkv_w2, rwkv_a0, rwkv_a2, rwkv_k_k,
           rwkv_k_a, rwkv_r_k, rwkv_gn_g, rwkv_gn_b, dsa_q_g, dsa_k_g, lru_conv_w, lru_conv_b,
           lru_w_r, lru_b_r, lru_w_i, lru_b_i, lru_lambda, w_branch, w_out, w_ple, w_ple_gate):
    B, S, D = x.shape
    depth = w_in.shape[0]
    T = B * S
    ksel = min(TOPK_MAX, S // 4)
    cosf, sinf = _rope_tables(positions)
    head_id = np.arange(D_BRANCH) // HEAD_DIM
    hsel = jnp.asarray(head_id[:, None] == head_id[None, :], BF16)
    row = lambda v: v.reshape(1, -1)
    per_head = lambda v: jnp.tile(v, N_HEADS).reshape(1, -1)

    h = x.reshape(T, D)
    for i in range(depth):
        u = _inproj(h, row(norm_g[i]), _reorder_w_in(w_in[i]))
        ya = _rwkv(u, B, S, row(rwkv_mu[i]), row(rwkv_w0[i]), rwkv_w2[i], row(rwkv_a0[i]),
                   rwkv_a2[i], row(rwkv_k_k[i]), row(rwkv_k_a[i]), row(rwkv_r_k[i]),
                   row(rwkv_gn_g[i]), row(rwkv_gn_b[i]), hsel)
        qn, kn, vb, qi, ki, wi = _dsa_prep(u, cosf, sinf, per_head(dsa_q_g[i]),
                                           per_head(dsa_k_g[i]), hsel)
        yb = _dsa(u, qn, kn, vb, qi, ki, wi, B, S, ksel)
        w_ri = jnp.concatenate([_block_diag(lru_w_r[i]), _block_diag(lru_w_i[i])], axis=1)
        b_ri = jnp.concatenate([lru_b_r[i], lru_b_i[i]]).reshape(1, -1)
        yc = _lru(u, B, S, lru_conv_w[i], row(lru_conv_b[i]), w_ri.astype(BF16), b_ri,
                  row(lru_lambda[i]))
        h = _merge(ya, yb, yc, u, h, p[i].reshape(T, PLE_DIM), w_branch[i].astype(BF16),
                   w_out[i].astype(BF16), w_ple[i].astype(BF16), w_ple_gate[i].astype(BF16))
    return h.reshape(B, S, D)
```

```python
import functools

import numpy as np
import jax
import jax.numpy as jnp
from jax import lax
from jax.experimental import pallas as pl
from jax.experimental.pallas import tpu as pltpu

F32 = jnp.float32
BF16 = jnp.bfloat16
I32 = jnp.int32

D_MODEL = 1024
D_BRANCH = 512
HEAD_DIM = 64
N_HEADS = 8
N_BRANCH = 3
PLE_DIM = 256
CHUNK = 64
CHUNK_SHIFT = 6
NORM_EPS = 1e-6
LORA = 64
RWKV_GN_EPS = 64e-5
IDX_HEADS = 4
IDX_DIM = 64
TOPK_MAX = 256
ROPE_THETA = 500000.0
ROPE_DIM = 16
CONV_WIDTH = 4
LRU_C = 8.0

LANES = 128
VMEM_LIMIT = 48 * 1024 * 1024

W_G = N_BRANCH * D_MODEL
W_C = 2 * D_BRANCH
W_BM = 4 * D_BRANCH
W_BI = 384
W_A = 4 * D_BRANCH + 2 * LORA
OFF_G, OFF_C, OFF_BM, OFF_BI, OFF_A = 0, 3072, 4096, 6144, 6528
D_IN_PAD = OFF_A + W_A

NEG_BIG = -1e30
INT_MIN = -2 ** 31


def _dot(a, b):
    return jnp.dot(a.astype(BF16), b.astype(BF16), preferred_element_type=F32)


def _dot_nt(a, b):
    return lax.dot_general(a.astype(BF16), b.astype(BF16), (((1,), (1,)), ((), ())),
                           preferred_element_type=F32)


SMALL_DOT_PIECES = 1


def _pieces(x):
    hi = x.astype(BF16)
    if SMALL_DOT_PIECES == 1:
        return [hi]
    return [hi, (x - hi.astype(F32)).astype(BF16)]


def _sdot_general(a, b, dims):
    pa, pb = _pieces(a), _pieces(b)
    out = None
    for i, x in enumerate(pa):
        for j, y in enumerate(pb):
            if i + j < SMALL_DOT_PIECES:
                t = lax.dot_general(x, y, (dims, ((), ())), preferred_element_type=F32)
                out = t if out is None else out + t
    return out


def _sdot(a, b):
    return _sdot_general(a, b, ((1,), (0,)))


def _sdot_nt(a, b):
    return _sdot_general(a, b, ((1,), (1,)))


def _dot_sel(x, sel_bf16):
    hi = x.astype(BF16)
    r1 = x - hi.astype(F32)
    mid = r1.astype(BF16)
    lo = (r1 - mid.astype(F32)).astype(BF16)
    out = jnp.dot(hi, sel_bf16, preferred_element_type=F32)
    out = out + jnp.dot(mid, sel_bf16, preferred_element_type=F32)
    return out + jnp.dot(lo, sel_bf16, preferred_element_type=F32)


def _params(*sem):
    return pltpu.CompilerParams(dimension_semantics=sem, vmem_limit_bytes=VMEM_LIMIT)


def _inproj_body(h_ref, g_ref, w_ref, o_ref, hn_ref):
    @pl.when(pl.program_id(1) == 0)
    def _():
        x = h_ref[...]
        ms = jnp.mean(x * x, axis=-1, keepdims=True)
        hn_ref[...] = (x * lax.rsqrt(ms + NORM_EPS) * g_ref[...]).astype(BF16)

    o_ref[...] = jnp.dot(hn_ref[...], w_ref[...], preferred_element_type=F32)


def _inproj(h, g, w):
    T, D = h.shape
    N = w.shape[1]
    tm, tn = min(1024, T), 512
    return pl.pallas_call(
        _inproj_body,
        grid=(T // tm, N // tn),
        in_specs=[pl.BlockSpec((tm, D), lambda i, j: (i, 0)),
                  pl.BlockSpec((1, D), lambda i, j: (0, 0)),
                  pl.BlockSpec((D, tn), lambda i, j: (0, j))],
        out_specs=pl.BlockSpec((tm, tn), lambda i, j: (i, j)),
        out_shape=jax.ShapeDtypeStruct((T, N), F32),
        scratch_shapes=[pltpu.VMEM((tm, D), BF16)],
        compiler_params=_params("parallel", "arbitrary"),
        name="inproj",
    )(h, g, w)


def _rwkv_body(ua_ref, mu_ref, w0_ref, w2_ref, a0_ref, a2_ref, kk_ref, ka_ref, rk_ref,
               gng_ref, gnb_ref, hsel_ref, o_ref, z_ref, prev_ref, obuf_ref):
    C = ua_ref.shape[0]
    DB = D_BRANCH

    @pl.when(pl.program_id(1) == 0)
    def _():
        z_ref[...] = jnp.zeros_like(z_ref)
        prev_ref[...] = jnp.zeros_like(prev_ref)

    ua = ua_ref[...]
    row = lax.broadcasted_iota(I32, (C, 1), 0)
    shifted = jnp.where(row == 0, prev_ref[...], pltpu.roll(ua, 1, 0))
    prev_ref[...] = ua[C - 1:C, :]
    xs = ua + (shifted - ua) * mu_ref[...]

    r = xs[:, 0:DB]
    k = xs[:, DB:2 * DB]
    v = xs[:, 2 * DB:3 * DB]
    wl = xs[:, 3 * DB:3 * DB + LORA]
    al = xs[:, 3 * DB + LORA:3 * DB + 2 * LORA]
    g = xs[:, 3 * DB + 2 * LORA:]
    hsel = hsel_ref[...]

    w_pre = w0_ref[...] + _dot(jnp.tanh(wl), w2_ref[...])
    lw = -jnp.exp(-jax.nn.softplus(-w_pre) - 0.5)
    a = jax.nn.sigmoid(a0_ref[...] + _dot(al, a2_ref[...]))
    kk = k * kk_ref[...]
    kk = kk / jnp.maximum(jnp.sqrt(_dot_sel(kk * kk, hsel)), 1e-12)
    k_mod = k * (1.0 + (a - 1.0) * ka_ref[...])
    b = kk * a

    ci = lax.broadcasted_iota(I32, (C, C), 0)
    cj = lax.broadcasted_iota(I32, (C, C), 1)
    incl = ci >= cj
    strict = ci > cj
    eye = ci == cj
    cl = _dot_sel_lhs(jnp.where(incl, 1.0, 0.0).astype(BF16), lw)
    cl_end = cl[C - 1:C, :]
    g_incl = jnp.exp(cl)
    g_inv = jnp.exp(-cl)
    d_end = jnp.exp(cl_end - cl)
    kt_all = kk * jnp.exp(cl - lw)
    rt_all = r * g_incl
    bs_all = b * g_inv
    ks_all = k_mod * g_inv
    bh_all = b * d_end
    kh_all = k_mod * d_end
    g_end = g_incl[C - 1:C, :]

    bh_t = bh_all.T
    kh_t = kh_all.T
    ident = jnp.where(eye, 1.0, 0.0)
    n_sq = int(np.log2(C)) - 1
    heads = range(N_HEADS)
    hsl = [slice(h * HEAD_DIM, (h + 1) * HEAD_DIM) for h in heads]
    kt = [kt_all[:, s] for s in hsl]
    rt = [rt_all[:, s] for s in hsl]
    vh = [v[:, s] for s in hsl]
    gram = [_sdot_nt(jnp.concatenate([kt[h], rt[h]], axis=0),
                     jnp.concatenate([bs_all[:, hsl[h]], ks_all[:, hsl[h]]], axis=0)) for h in heads]
    b_m = [jnp.where(strict, gram[h][0:C, C:2 * C], 0.0) for h in heads]
    ab = [jnp.where(incl, gram[h][C:2 * C, 0:C], 0.0) for h in heads]
    bk = [jnp.where(incl, gram[h][C:2 * C, C:2 * C], 0.0) for h in heads]
    pw = [jnp.where(strict, -gram[h][0:C, 0:C], 0.0) for h in heads]
    tinv = [ident + pw[h] for h in heads]
    bmv = [_sdot(b_m[h], vh[h]) for h in heads]
    nv = [_sdot(jnp.concatenate([kh_t[hsl[h], :], bk[h]], axis=0), vh[h]) for h in heads]
    for _ in range(n_sq):
        pw = [_sdot(pw[h], pw[h]) for h in heads]
        tinv = [_sdot(tinv[h], ident + pw[h]) for h in heads]
    wu = [-_sdot(tinv[h], jnp.concatenate([kt[h], bmv[h]], axis=1)) for h in heads]
    mq = [_sdot(jnp.concatenate([bh_t[hsl[h], :], ab[h]], axis=0), wu[h]) for h in heads]
    lhs = []
    for h in heads:
        m_m = jnp.where(eye, g_end[:, hsl[h]], 0.0) + mq[h][0:HEAD_DIM, 0:HEAD_DIM]
        qe = rt[h] + mq[h][HEAD_DIM:, 0:HEAD_DIM]
        lhs.append(jnp.concatenate([qe, m_m], axis=0))
    oz = [_sdot(lhs[h], z_ref[h]) for h in heads]
    for h in heads:
        n_m = mq[h][0:HEAD_DIM, HEAD_DIM:] + nv[h][0:HEAD_DIM, :]
        o0 = mq[h][HEAD_DIM:, HEAD_DIM:] + nv[h][HEAD_DIM:, :]
        obuf_ref[:, hsl[h]] = oz[h][0:C, :] + o0
        z_ref[h] = oz[h][C:, :] + n_m

    o = obuf_ref[...]
    inv_n = 1.0 / HEAD_DIM
    mean = _dot_sel(o, hsel) * inv_n
    oc = o - mean
    var = _dot_sel(oc * oc, hsel) * inv_n
    o = oc * lax.rsqrt(var + RWKV_GN_EPS) * gng_ref[...] + gnb_ref[...]
    bonus = _dot_sel(r * k_mod * rk_ref[...], hsel) * v
    o_ref[...] = ((o + bonus) * (g * jax.nn.sigmoid(g))).astype(o_ref.dtype)


def _dot_sel_lhs(sel_bf16, x):
    hi = x.astype(BF16)
    r1 = x - hi.astype(F32)
    mid = r1.astype(BF16)
    lo = (r1 - mid.astype(F32)).astype(BF16)
    out = jnp.dot(sel_bf16, hi, preferred_element_type=F32)
    out = out + jnp.dot(sel_bf16, mid, preferred_element_type=F32)
    return out + jnp.dot(sel_bf16, lo, preferred_element_type=F32)


def _rwkv(u, B, S, mu, w0, w2, a0, a2, k_k, k_a, r_k, gn_g, gn_b, hsel):
    T = u.shape[0]
    C = CHUNK
    nc = S // C
    a_blk = OFF_A // W_A
    vec = lambda n: pl.BlockSpec((1, n), lambda b, c: (0, 0))
    mat = lambda m, n: pl.BlockSpec((m, n), lambda b, c: (0, 0))
    return pl.pallas_call(
        _rwkv_body,
        grid=(B, nc),
        in_specs=[pl.BlockSpec((C, W_A), lambda b, c: (b * nc + c, a_blk)),
                  vec(W_A), vec(D_BRANCH), mat(LORA, D_BRANCH), vec(D_BRANCH), mat(LORA, D_BRANCH),
                  vec(D_BRANCH), vec(D_BRANCH), vec(D_BRANCH), vec(D_BRANCH), vec(D_BRANCH),
                  mat(D_BRANCH, D_BRANCH)],
        out_specs=pl.BlockSpec((C, D_BRANCH), lambda b, c: (b * nc + c, 0)),
        out_shape=jax.ShapeDtypeStruct((T, D_BRANCH), BF16),
        scratch_shapes=[pltpu.VMEM((N_HEADS, HEAD_DIM, HEAD_DIM), F32),
                        pltpu.VMEM((1, W_A), F32),
                        pltpu.VMEM((C, D_BRANCH), F32)],
        compiler_params=_params("parallel", "arbitrary"),
        name="rwkv",
    )(u, mu, w0, w2, a0, a2, k_k, k_a, r_k, gn_g, gn_b, hsel)


def _rope(x, cosw, sinw):
    W = x.shape[1]
    lane = lax.broadcasted_iota(I32, (1, W), 1) & (HEAD_DIM - 1)
    half = ROPE_DIM // 2
    partner = jnp.where(lane < half, pltpu.roll(x, W - half, 1), pltpu.roll(x, half, 1))
    return x * cosw + partner * sinw


def _dsa_prep_body(um_ref, ui_ref, cos_ref, sin_ref, qg_ref, kg_ref, hsel_ref,
                   qn_ref, kn_ref, vt_ref, qi_ref, ki_ref, wt_ref):
    DB = D_BRANCH
    cos1, sin1 = cos_ref[...], sin_ref[...]
    cos2, sin2 = jnp.concatenate([cos1] * 2, axis=1), jnp.concatenate([sin1] * 2, axis=1)
    cos4, sin4 = jnp.concatenate([cos2] * 2, axis=1), jnp.concatenate([sin2] * 2, axis=1)
    hsel = hsel_ref[...]

    def headnorm(x, gain):
        ms = _dot_sel(x * x, hsel) * (1.0 / HEAD_DIM)
        return x * lax.rsqrt(ms + NORM_EPS) * gain

    q = _rope(headnorm(um_ref[:, 0:DB], qg_ref[...]), cos4, sin4)
    k = _rope(headnorm(um_ref[:, DB:2 * DB], kg_ref[...]), cos4, sin4)
    qn_ref[...] = (q * (HEAD_DIM ** -0.5)).astype(BF16)
    kn_ref[...] = k.astype(BF16)
    vt_ref[...] = um_ref[:, 2 * DB:3 * DB].T.astype(BF16)
    qi_ref[...] = _rope(ui_ref[:, 0:IDX_HEADS * IDX_DIM], cos2, sin2).astype(BF16)
    kw = ui_ref[:, IDX_HEADS * IDX_DIM:]
    ki_ref[...] = _rope(kw, cos1, sin1)[:, 0:IDX_DIM].astype(BF16)
    wt_ref[...] = (kw * (IDX_HEADS ** -0.5 * IDX_DIM ** -0.5)).T


def _dsa_prep(u, B, S, cosf, sinf, qg, kg, hsel):
    T = u.shape[0]
    tm = min(512, S)
    ns = S // tm
    vec = lambda n: pl.BlockSpec((1, n), lambda i: (0, 0))
    row = lambda n: pl.BlockSpec((tm, n), lambda i: (i, 0))
    col = lambda n: pl.BlockSpec((None, n, tm), lambda i: (i // ns, 0, i % ns))
    return pl.pallas_call(
        _dsa_prep_body,
        grid=(T // tm,),
        in_specs=[pl.BlockSpec((tm, W_BM), lambda i: (i, OFF_BM // W_BM)),
                  pl.BlockSpec((tm, W_BI), lambda i: (i, OFF_BI // W_BI)),
                  row(LANES), row(LANES), vec(D_BRANCH), vec(D_BRANCH),
                  pl.BlockSpec((D_BRANCH, D_BRANCH), lambda i: (0, 0))],
        out_specs=[row(D_BRANCH), row(D_BRANCH), col(D_BRANCH), row(IDX_HEADS * IDX_DIM),
                   row(IDX_DIM), col(LANES)],
        out_shape=[jax.ShapeDtypeStruct((T, D_BRANCH), BF16),
                   jax.ShapeDtypeStruct((T, D_BRANCH), BF16),
                   jax.ShapeDtypeStruct((B, D_BRANCH, S), BF16),
                   jax.ShapeDtypeStruct((T, IDX_HEADS * IDX_DIM), BF16),
                   jax.ShapeDtypeStruct((T, IDX_DIM), BF16),
                   jax.ShapeDtypeStruct((B, LANES, S), F32)],
        compiler_params=_params("parallel"),
        name="dsa_prep",
    )(u, u, cosf, sinf, qg, kg, hsel)


def _dsa_body(qn_ref, qi_ref, wt_ref, g_ref, kn_ref, vt_ref, ki_ref, o_ref,
              key_ref, m_ref, l_ref, acc_ref, *, ksel, kt):
    QB = qn_ref.shape[0]
    S = kn_ref.shape[0]
    SUB = 8
    q0 = pl.program_id(1) * QB
    qlane = lax.broadcasted_iota(I32, (1, QB), 1)
    lim = (((q0 + qlane) >> CHUNK_SHIFT) + 1) << CHUNK_SHIFT
    n_t = (q0 + QB + kt - 1) // kt
    krow = lax.broadcasted_iota(I32, (kt, 1), 0)
    imin = jnp.int32(INT_MIN)
    qi = qi_ref[...]
    qi_h = [qi[:, h * IDX_DIM:(h + 1) * IDX_DIM] for h in range(IDX_HEADS)]
    w_h = [wt_ref[IDX_DIM + h:IDX_DIM + h + 1, :] for h in range(IDX_HEADS)]

    def fold(x, op):
        return op(x.reshape(kt // SUB, SUB, QB), axis=0)

    def score_tile(t, carry):
        off = pl.multiple_of(t * kt, kt)
        ki_t = ki_ref[pl.ds(off, kt), :]
        sc = jnp.zeros((kt, QB), F32)
        for h in range(IDX_HEADS):
            d = lax.dot_general(ki_t, qi_h[h], (((1,), (1,)), ((), ())),
                                preferred_element_type=F32)
            sc = sc + w_h[h] * jnp.maximum(d, 0.0)
        bits = pltpu.bitcast(sc, I32)
        key = bits ^ ((bits >> 31) & jnp.int32(0x7FFFFFFF))
        key_ref[pl.ds(off, kt), :] = jnp.where(off + krow < lim, key, imin)
        return carry

    lax.fori_loop(0, n_t, score_tile, 0)

    def count(pred):
        def body(t, acc):
            off = pl.multiple_of(t * kt, kt)
            hit = jnp.where(pred(key_ref[pl.ds(off, kt), :], off), 1, 0)
            return acc + fold(hit, jnp.sum)

        acc = lax.fori_loop(0, n_t, body, jnp.zeros((SUB, QB), I32))
        return jnp.sum(acc, axis=0, keepdims=True)

    def bisect(i, lo):
        cand = lo + jnp.left_shift(jnp.int32(1), 31 - i)
        cnt = count(lambda key, off: key >= cand)
        return jnp.where(cnt >= ksel, cand, lo)

    tau = lax.fori_loop(0, 32, bisect, jnp.full((1, QB), INT_MIN, I32))

    need = ksel - count(lambda key, off: key > tau)
    n_bits = int(np.log2(S))

    def tie_bisect(i, jlo):
        cand = jlo + jnp.left_shift(jnp.int32(1), n_bits - 1 - i)
        cnt = count(lambda key, off: jnp.where(key == tau, off + krow, S) < cand)
        return jnp.where(cnt < need, cand, jlo)

    has_ties = jnp.max(count(lambda key, off: key >= tau)) > ksel
    jmax = lax.cond(has_ties,
                    lambda: lax.fori_loop(0, n_bits, tie_bisect, jnp.zeros((1, QB), I32)),
                    lambda: jnp.full((1, QB), S, I32))

    m_ref[...] = jnp.full_like(m_ref, NEG_BIG)
    l_ref[...] = jnp.zeros_like(l_ref)
    acc_ref[...] = jnp.zeros_like(acc_ref)
    low_half = lax.broadcasted_iota(I32, (1, LANES), 1) < HEAD_DIM
    qn = qn_ref[...].astype(F32)
    q_pair = []
    for h in range(N_HEADS):
        pair = qn[:, (h // 2) * LANES:(h // 2 + 1) * LANES]
        own = low_half if h % 2 == 0 else jnp.logical_not(low_half)
        q_pair.append(jnp.where(own, pair, 0.0).astype(BF16))
    ONES_ROWS = 16
    ones = jnp.ones((ONES_ROWS, kt), BF16)

    def att_tile(t, carry):
        off = pl.multiple_of(t * kt, kt)
        key = key_ref[pl.ds(off, kt), :]
        tie_ok = jnp.where(key == tau, off + krow, S) <= jmax
        bias = jnp.where(key > tau, 0.0, jnp.where(tie_ok, 0.0, NEG_BIG))
        bias = jnp.where(key == imin, NEG_BIG, bias)
        heads = range(N_HEADS)
        pairs = [slice(pr * LANES, (pr + 1) * LANES) for pr in range(N_HEADS // 2)]
        k_p = [kn_ref[pl.ds(off, kt), ps] for ps in pairs]
        vo_p = [jnp.concatenate([vt_ref[ps, pl.ds(off, kt)], ones], axis=0) for ps in pairs]
        s = [lax.dot_general(k_p[h // 2], q_pair[h], (((1,), (1,)), ((), ())),
                             preferred_element_type=F32) + bias for h in heads]
        m_old = [m_ref[h] for h in heads]
        m_new = [jnp.maximum(m_old[h], jnp.max(fold(s[h], jnp.max), axis=0, keepdims=True))
                 for h in heads]
        p = [jnp.exp(s[h] - m_new[h][0:1, :]).astype(BF16) for h in heads]
        alpha = [jnp.exp(m_old[h] - m_new[h]) for h in heads]
        pvl = [jnp.dot(vo_p[h // 2], p[h], preferred_element_type=F32) for h in heads]
        for h in heads:
            l_ref[h] = alpha[h] * l_ref[h] + pvl[h][LANES:LANES + SUB, :]
            m_ref[h] = m_new[h]
            hs = slice(h * HEAD_DIM, (h + 1) * HEAD_DIM)
            own = slice((h % 2) * HEAD_DIM, (h % 2 + 1) * HEAD_DIM)
            acc_ref[hs, :] = alpha[h][0:1, :] * acc_ref[hs, :] + pvl[h][own, :]
        return carry

    lax.fori_loop(0, n_t, att_tile, 0)

    o_t = jnp.concatenate(
        [acc_ref[h * HEAD_DIM:(h + 1) * HEAD_DIM, :] / l_ref[h][0:1, :] for h in range(N_HEADS)],
        axis=0)
    g = g_ref[...]
    o_ref[...] = (o_t.T * (g * jax.nn.sigmoid(g))).astype(o_ref.dtype)


def _dsa(u, qn, kn, vt, qi, ki, wt, B, S, ksel):
    T = u.shape[0]
    QB = min(256, S)
    kt = min(512, S)
    nq = S // QB
    g_blk = (OFF_BM + 3 * D_BRANCH) // D_BRANCH
    qrow = lambda n: pl.BlockSpec((QB, n), lambda b, i: (b * nq + i, 0))
    full = lambda n: pl.BlockSpec((S, n), lambda b, i: (b, 0))
    return pl.pallas_call(
        functools.partial(_dsa_body, ksel=ksel, kt=kt),
        grid=(B, nq),
        in_specs=[qrow(D_BRANCH), qrow(IDX_HEADS * IDX_DIM),
                  pl.BlockSpec((None, LANES, QB), lambda b, i: (b, 0, i)),
                  pl.BlockSpec((QB, D_BRANCH), lambda b, i: (b * nq + i, g_blk)),
                  full(D_BRANCH),
                  pl.BlockSpec((None, D_BRANCH, S), lambda b, i: (b, 0, 0)),
                  full(IDX_DIM)],
        out_specs=qrow(D_BRANCH),
        out_shape=jax.ShapeDtypeStruct((T, D_BRANCH), BF16),
        scratch_shapes=[pltpu.VMEM((S, QB), I32),
                        pltpu.VMEM((N_HEADS, 8, QB), F32),
                        pltpu.VMEM((N_HEADS, 8, QB), F32),
                        pltpu.VMEM((D_BRANCH, QB), F32)],
        compiler_params=_params("parallel", "arbitrary"),
        name="dsa",
    )(qn, qi, wt, u, kn, vt, ki)


def _lru_body(u_ref, cw_ref, cb_ref, wri_ref, bri_ref, lam_ref, o_ref, xbuf_ref, h_ref):
    TS = u_ref.shape[0]
    DB = D_BRANCH
    PAD = 8

    @pl.when(pl.program_id(1) == 0)
    def _():
        xbuf_ref[0:PAD, :] = jnp.zeros((PAD, DB), F32)
        h_ref[...] = jnp.zeros_like(h_ref)

    x = u_ref[:, 0:DB]
    g = u_ref[:, DB:2 * DB]
    xbuf_ref[PAD:PAD + TS, :] = x
    xc = cb_ref[...]
    for j in range(CONV_WIDTH):
        xc = xc + cw_ref[j:j + 1, :] * xbuf_ref[pl.ds(PAD - (CONV_WIDTH - 1) + j, TS), :]
    xbuf_ref[0:PAD, :] = x[TS - PAD:TS, :]

    ri = jax.nn.sigmoid(_dot(xc, wri_ref[...]) + bri_ref[...])
    r, i = ri[:, 0:DB], ri[:, DB:2 * DB]
    log_a = (-LRU_C) * r * jax.nn.softplus(-lam_ref[...])
    a = jnp.exp(log_a)
    b = jnp.sqrt(jnp.tanh(-log_a) * (1.0 + a * a)) * (i * xc)

    row = lax.broadcasted_iota(I32, (TS, 1), 0)
    step = 1
    while step < TS:
        live = row >= step
        a_s = jnp.where(live, pltpu.roll(a, step, 0), 1.0)
        b_s = jnp.where(live, pltpu.roll(b, step, 0), 0.0)
        b = a * b_s + b
        a = a * a_s
        step *= 2
    h = b + a * h_ref[...]
    h_ref[...] = h[TS - 1:TS, :]
    o_ref[...] = (h * (g * jax.nn.sigmoid(g))).astype(o_ref.dtype)


def _lru(u, B, S, conv_w, conv_b, w_ri, b_ri, lam):
    T = u.shape[0]
    TS = min(256, S)
    ns = S // TS
    vec = lambda n: pl.BlockSpec((1, n), lambda b, s: (0, 0))
    return pl.pallas_call(
        _lru_body,
        grid=(B, ns),
        in_specs=[pl.BlockSpec((TS, W_C), lambda b, s: (b * ns + s, OFF_C // W_C)),
                  pl.BlockSpec((CONV_WIDTH, D_BRANCH), lambda b, s: (0, 0)),
                  vec(D_BRANCH),
                  pl.BlockSpec((D_BRANCH, 2 * D_BRANCH), lambda b, s: (0, 0)),
                  vec(2 * D_BRANCH), vec(D_BRANCH)],
        out_specs=pl.BlockSpec((TS, D_BRANCH), lambda b, s: (b * ns + s, 0)),
        out_shape=jax.ShapeDtypeStruct((T, D_BRANCH), BF16),
        scratch_shapes=[pltpu.VMEM((TS + 8, D_BRANCH), F32), pltpu.VMEM((1, D_BRANCH), F32)],
        compiler_params=_params("parallel", "arbitrary"),
        name="lru",
    )(u, conv_w, conv_b, w_ri, b_ri, lam)


def _merge_body(ya_ref, yb_ref, yc_ref, ug_ref, h_ref, p_ref, wb_ref, wo_ref, wp_ref, wpg_ref,
                o_ref):
    D = D_MODEL
    merged = None
    for n, y_ref in enumerate((ya_ref, yb_ref, yc_ref)):
        term = jax.nn.sigmoid(ug_ref[:, n * D:(n + 1) * D]) * _dot(y_ref[...], wb_ref[n])
        merged = term if merged is None else merged + term
    h1 = h_ref[...] + _dot(merged, wo_ref[...])
    gate = jax.nn.sigmoid(_dot(h1, wpg_ref[...]))
    o_ref[...] = h1 + gate * _dot(p_ref[...], wp_ref[...])


def _merge(ya, yb, yc, u, h, p, wb, wo, wp, wpg):
    T, D = h.shape
    tm = min(256, T)
    row = lambda n: pl.BlockSpec((tm, n), lambda i: (i, 0))
    return pl.pallas_call(
        _merge_body,
        grid=(T // tm,),
        in_specs=[row(D_BRANCH), row(D_BRANCH), row(D_BRANCH),
                  pl.BlockSpec((tm, W_G), lambda i: (i, OFF_G // W_G)),
                  row(D), row(PLE_DIM),
                  pl.BlockSpec((N_BRANCH, D_BRANCH, D), lambda i: (0, 0, 0)),
                  pl.BlockSpec((D, D), lambda i: (0, 0)),
                  pl.BlockSpec((PLE_DIM, D), lambda i: (0, 0)),
                  pl.BlockSpec((D, D), lambda i: (0, 0))],
        out_specs=row(D),
        out_shape=jax.ShapeDtypeStruct((T, D), F32),
        compiler_params=_params("parallel"),
        name="merge",
    )(ya, yb, yc, u, h, p, wb, wo, wp, wpg)


def _reorder_w_in(w):
    a0 = 0
    b0 = a0 + W_A
    c0 = b0 + 3 * D_BRANCH + IDX_HEADS * IDX_DIM + IDX_DIM + IDX_HEADS + D_BRANCH
    g0 = c0 + W_C
    bq = w[:, b0:b0 + 3 * D_BRANCH]
    bi0 = b0 + 3 * D_BRANCH
    n_idx = IDX_HEADS * IDX_DIM + IDX_DIM + IDX_HEADS
    bidx = w[:, bi0:bi0 + n_idx]
    bg = w[:, bi0 + n_idx:c0]
    pad = jnp.zeros((w.shape[0], W_BI - n_idx), w.dtype)
    out = jnp.concatenate([w[:, g0:g0 + W_G], w[:, c0:g0], bq, bg, bidx, pad, w[:, a0:b0]], axis=1)
    return _to_bf16(out)


def _cast_body(x_ref, o_ref):
    o_ref[...] = x_ref[...].astype(o_ref.dtype)


def _to_bf16(w):
    R, N = w.shape
    tn = 512
    return pl.pallas_call(
        _cast_body,
        grid=(N // tn,),
        in_specs=[pl.BlockSpec((R, tn), lambda j: (0, j))],
        out_specs=pl.BlockSpec((R, tn), lambda j: (0, j)),
        out_shape=jax.ShapeDtypeStruct((R, N), BF16),
        compiler_params=_params("parallel"),
        name="cast_bf16",
    )(w)


def _block_diag(w):
    H, n, _ = w.shape
    eye = jnp.eye(H, dtype=w.dtype)
    return (eye[:, None, :, None] * w[:, :, None, :]).reshape(H * n, H * n)


def _rope_tables(positions):
    half = ROPE_DIM // 2
    inv_freq = ROPE_THETA ** (-jnp.arange(0, ROPE_DIM, 2, dtype=F32) / ROPE_DIM)
    ang = positions.astype(F32).reshape(-1, 1) * inv_freq
    cos, sin = jnp.cos(ang), jnp.sin(ang)
    T = cos.shape[0]
    rest = HEAD_DIM - ROPE_DIM
    cos_h = jnp.concatenate([cos, cos, jnp.ones((T, rest), F32)], axis=1)
    sin_h = jnp.concatenate([-sin, sin, jnp.zeros((T, rest), F32)], axis=1)
    return jnp.tile(cos_h, (1, LANES // HEAD_DIM)), jnp.tile(sin_h, (1, LANES // HEAD_DIM))


def kernel(x, p, positions, norm_g, w_in, rwkv_mu, rwkv_w0, rwkv_w2, rwkv_a0, rwkv_a2, rwkv_k_k,
           rwkv_k_a, rwkv_r_k, rwkv_gn_g, rwkv_gn_b, dsa_q_g, dsa_k_g, lru_conv_w, lru_conv_b,
           lru_w_r, lru_b_r, lru_w_i, lru_b_i, lru_lambda, w_branch, w_out, w_ple, w_ple_gate):
    B, S, D = x.shape
    depth = w_in.shape[0]
    T = B * S
    ksel = min(TOPK_MAX, S // 4)
    cosf, sinf = _rope_tables(positions)
    head_id = np.arange(D_BRANCH) // HEAD_DIM
    hsel = jnp.asarray(head_id[:, None] == head_id[None, :], BF16)
    row = lambda v: v.reshape(1, -1)
    per_head = lambda v: jnp.tile(v, N_HEADS).reshape(1, -1)

    h = x.reshape(T, D)
    for i in range(depth):
        u = _inproj(h, row(norm_g[i]), _reorder_w_in(w_in[i]))
        ya = _rwkv(u, B, S, row(rwkv_mu[i]), row(rwkv_w0[i]), rwkv_w2[i], row(rwkv_a0[i]),
                   rwkv_a2[i], row(rwkv_k_k[i]), row(rwkv_k_a[i]), row(rwkv_r_k[i]),
                   row(rwkv_gn_g[i]), row(rwkv_gn_b[i]), hsel)
        qn, kn, vt, qi, ki, wt = _dsa_prep(u, B, S, cosf, sinf, per_head(dsa_q_g[i]),
                                           per_head(dsa_k_g[i]), hsel)
        yb = _dsa(u, qn, kn, vt, qi, ki, wt, B, S, ksel)
        w_ri = jnp.concatenate([_block_diag(lru_w_r[i]), _block_diag(lru_w_i[i])], axis=1)
        b_ri = jnp.concatenate([lru_b_r[i], lru_b_i[i]]).reshape(1, -1)
        yc = _lru(u, B, S, lru_conv_w[i], row(lru_conv_b[i]), w_ri.astype(BF16), b_ri,
                  row(lru_lambda[i]))
        h = _merge(ya, yb, yc, u, h, p[i].reshape(T, PLE_DIM), w_branch[i].astype(BF16),
                   w_out[i].astype(BF16), w_ple[i].astype(BF16), w_ple_gate[i].astype(BF16))
    return h.reshape(B, S, D)
```

```python
import functools

import numpy as np
import jax
import jax.numpy as jnp
from jax import lax
from jax.experimental import pallas as pl
from jax.experimental.pallas import tpu as pltpu

F32 = jnp.float32
BF16 = jnp.bfloat16
I32 = jnp.int32
I16 = jnp.int16
HALF = 1 << 15
PACK = 16

D_MODEL = 1024
D_BRANCH = 512
HEAD_DIM = 64
N_HEADS = 8
N_BRANCH = 3
PLE_DIM = 256
CHUNK = 64
CHUNK_SHIFT = 6
NORM_EPS = 1e-6
LORA = 64
RWKV_GN_EPS = 64e-5
IDX_HEADS = 4
IDX_DIM = 64
TOPK_MAX = 256
ROPE_THETA = 500000.0
ROPE_DIM = 16
CONV_WIDTH = 4
LRU_C = 8.0

LANES = 128
VMEM_LIMIT = 48 * 1024 * 1024

W_G = N_BRANCH * D_MODEL
W_C = 2 * D_BRANCH
W_BM = 4 * D_BRANCH
W_BI = 384
W_A = 4 * D_BRANCH + 2 * LORA
OFF_G, OFF_C, OFF_BM, OFF_BI, OFF_A = 0, 3072, 4096, 6144, 6528
D_IN_PAD = OFF_A + W_A

NEG_BIG = -1e30
INT_MIN = -2 ** 31


def _dot(a, b):
    return jnp.dot(a.astype(BF16), b.astype(BF16), preferred_element_type=F32)


def _dot_nt(a, b):
    return lax.dot_general(a.astype(BF16), b.astype(BF16), (((1,), (1,)), ((), ())),
                           preferred_element_type=F32)


SMALL_DOT_PIECES = 1


def _pieces(x):
    hi = x.astype(BF16)
    if SMALL_DOT_PIECES == 1:
        return [hi]
    return [hi, (x - hi.astype(F32)).astype(BF16)]


def _sdot_general(a, b, dims):
    pa, pb = _pieces(a), _pieces(b)
    out = None
    for i, x in enumerate(pa):
        for j, y in enumerate(pb):
            if i + j < SMALL_DOT_PIECES:
                t = lax.dot_general(x, y, (dims, ((), ())), preferred_element_type=F32)
                out = t if out is None else out + t
    return out


def _sdot(a, b):
    return _sdot_general(a, b, ((1,), (0,)))


def _sdot_nt(a, b):
    return _sdot_general(a, b, ((1,), (1,)))


def _dot_sel(x, sel_bf16):
    hi = x.astype(BF16)
    r1 = x - hi.astype(F32)
    mid = r1.astype(BF16)
    lo = (r1 - mid.astype(F32)).astype(BF16)
    out = jnp.dot(hi, sel_bf16, preferred_element_type=F32)
    out = out + jnp.dot(mid, sel_bf16, preferred_element_type=F32)
    return out + jnp.dot(lo, sel_bf16, preferred_element_type=F32)


def _params(*sem):
    return pltpu.CompilerParams(dimension_semantics=sem, vmem_limit_bytes=VMEM_LIMIT)


def _inproj_body(h_ref, g_ref, w_ref, o_ref, hn_ref):
    @pl.when(pl.program_id(1) == 0)
    def _():
        x = h_ref[...]
        ms = jnp.mean(x * x, axis=-1, keepdims=True)
        hn_ref[...] = (x * lax.rsqrt(ms + NORM_EPS) * g_ref[...]).astype(BF16)

    o_ref[...] = jnp.dot(hn_ref[...], w_ref[...], preferred_element_type=F32)


def _inproj(h, g, w):
    T, D = h.shape
    N = w.shape[1]
    tm, tn = min(1024, T), 512
    return pl.pallas_call(
        _inproj_body,
        grid=(T // tm, N // tn),
        in_specs=[pl.BlockSpec((tm, D), lambda i, j: (i, 0)),
                  pl.BlockSpec((1, D), lambda i, j: (0, 0)),
                  pl.BlockSpec((D, tn), lambda i, j: (0, j))],
        out_specs=pl.BlockSpec((tm, tn), lambda i, j: (i, j)),
        out_shape=jax.ShapeDtypeStruct((T, N), F32),
        scratch_shapes=[pltpu.VMEM((tm, D), BF16)],
        compiler_params=_params("parallel", "arbitrary"),
        name="inproj",
    )(h, g, w)


def _rwkv_body(ua_ref, mu_ref, w0_ref, w2_ref, a0_ref, a2_ref, kk_ref, ka_ref, rk_ref,
               gng_ref, gnb_ref, hsel_ref, o_ref, z_ref, prev_ref, obuf_ref):
    C = ua_ref.shape[0]
    DB = D_BRANCH

    @pl.when(pl.program_id(1) == 0)
    def _():
        z_ref[...] = jnp.zeros_like(z_ref)
        prev_ref[...] = jnp.zeros_like(prev_ref)

    ua = ua_ref[...]
    row = lax.broadcasted_iota(I32, (C, 1), 0)
    shifted = jnp.where(row == 0, prev_ref[...], pltpu.roll(ua, 1, 0))
    prev_ref[...] = ua[C - 1:C, :]
    xs = ua + (shifted - ua) * mu_ref[...]

    r = xs[:, 0:DB]
    k = xs[:, DB:2 * DB]
    v = xs[:, 2 * DB:3 * DB]
    wl = xs[:, 3 * DB:3 * DB + LORA]
    al = xs[:, 3 * DB + LORA:3 * DB + 2 * LORA]
    g = xs[:, 3 * DB + 2 * LORA:]
    hsel = hsel_ref[...]

    w_pre = w0_ref[...] + _dot(jnp.tanh(wl), w2_ref[...])
    lw = -jnp.exp(-jax.nn.softplus(-w_pre) - 0.5)
    a = jax.nn.sigmoid(a0_ref[...] + _dot(al, a2_ref[...]))
    kk = k * kk_ref[...]
    kk = kk / jnp.maximum(jnp.sqrt(_dot_sel(kk * kk, hsel)), 1e-12)
    k_mod = k * (1.0 + (a - 1.0) * ka_ref[...])
    b = kk * a

    ci = lax.broadcasted_iota(I32, (C, C), 0)
    cj = lax.broadcasted_iota(I32, (C, C), 1)
    incl = ci >= cj
    strict = ci > cj
    eye = ci == cj
    cl = _dot_sel_lhs(jnp.where(incl, 1.0, 0.0).astype(BF16), lw)
    cl_end = cl[C - 1:C, :]
    g_incl = jnp.exp(cl)
    g_inv = jnp.exp(-cl)
    d_end = jnp.exp(cl_end - cl)
    kt_all = kk * jnp.exp(cl - lw)
    rt_all = r * g_incl
    bs_all = b * g_inv
    ks_all = k_mod * g_inv
    bh_all = b * d_end
    kh_all = k_mod * d_end
    g_end = g_incl[C - 1:C, :]

    bh_t = bh_all.T
    kh_t = kh_all.T
    ident = jnp.where(eye, 1.0, 0.0)
    n_sq = int(np.log2(C)) - 1
    heads = range(N_HEADS)
    hsl = [slice(h * HEAD_DIM, (h + 1) * HEAD_DIM) for h in heads]
    kt = [kt_all[:, s] for s in hsl]
    rt = [rt_all[:, s] for s in hsl]
    vh = [v[:, s] for s in hsl]
    gram = [_sdot_nt(jnp.concatenate([kt[h], rt[h]], axis=0),
                     jnp.concatenate([bs_all[:, hsl[h]], ks_all[:, hsl[h]]], axis=0)) for h in heads]
    b_m = [jnp.where(strict, gram[h][0:C, C:2 * C], 0.0) for h in heads]
    ab = [jnp.where(incl, gram[h][C:2 * C, 0:C], 0.0) for h in heads]
    bk = [jnp.where(incl, gram[h][C:2 * C, C:2 * C], 0.0) for h in heads]
    pw = [jnp.where(strict, -gram[h][0:C, 0:C], 0.0) for h in heads]
    tinv = [ident + pw[h] for h in heads]
    bmv = [_sdot(b_m[h], vh[h]) for h in heads]
    nv = [_sdot(jnp.concatenate([kh_t[hsl[h], :], bk[h]], axis=0), vh[h]) for h in heads]
    for _ in range(n_sq):
        pw = [_sdot(pw[h], pw[h]) for h in heads]
        tinv = [_sdot(tinv[h], ident + pw[h]) for h in heads]
    wu = [-_sdot(tinv[h], jnp.concatenate([kt[h], bmv[h]], axis=1)) for h in heads]
    mq = [_sdot(jnp.concatenate([bh_t[hsl[h], :], ab[h]], axis=0), wu[h]) for h in heads]
    lhs = []
    for h in heads:
        m_m = jnp.where(eye, g_end[:, hsl[h]], 0.0) + mq[h][0:HEAD_DIM, 0:HEAD_DIM]
        qe = rt[h] + mq[h][HEAD_DIM:, 0:HEAD_DIM]
        lhs.append(jnp.concatenate([qe, m_m], axis=0))
    oz = [_sdot(lhs[h], z_ref[h]) for h in heads]
    for h in heads:
        n_m = mq[h][0:HEAD_DIM, HEAD_DIM:] + nv[h][0:HEAD_DIM, :]
        o0 = mq[h][HEAD_DIM:, HEAD_DIM:] + nv[h][HEAD_DIM:, :]
        obuf_ref[:, hsl[h]] = oz[h][0:C, :] + o0
        z_ref[h] = oz[h][C:, :] + n_m

    o = obuf_ref[...]
    inv_n = 1.0 / HEAD_DIM
    mean = _dot_sel(o, hsel) * inv_n
    oc = o - mean
    var = _dot_sel(oc * oc, hsel) * inv_n
    o = oc * lax.rsqrt(var + RWKV_GN_EPS) * gng_ref[...] + gnb_ref[...]
    bonus = _dot_sel(r * k_mod * rk_ref[...], hsel) * v
    o_ref[...] = ((o + bonus) * (g * jax.nn.sigmoid(g))).astype(o_ref.dtype)


def _dot_sel_lhs(sel_bf16, x):
    hi = x.astype(BF16)
    r1 = x - hi.astype(F32)
    mid = r1.astype(BF16)
    lo = (r1 - mid.astype(F32)).astype(BF16)
    out = jnp.dot(sel_bf16, hi, preferred_element_type=F32)
    out = out + jnp.dot(sel_bf16, mid, preferred_element_type=F32)
    return out + jnp.dot(sel_bf16, lo, preferred_element_type=F32)


def _rwkv(u, B, S, mu, w0, w2, a0, a2, k_k, k_a, r_k, gn_g, gn_b, hsel):
    T = u.shape[0]
    C = CHUNK
    nc = S // C
    a_blk = OFF_A // W_A
    vec = lambda n: pl.BlockSpec((1, n), lambda b, c: (0, 0))
    mat = lambda m, n: pl.BlockSpec((m, n), lambda b, c: (0, 0))
    return pl.pallas_call(
        _rwkv_body,
        grid=(B, nc),
        in_specs=[pl.BlockSpec((C, W_A), lambda b, c: (b * nc + c, a_blk)),
                  vec(W_A), vec(D_BRANCH), mat(LORA, D_BRANCH), vec(D_BRANCH), mat(LORA, D_BRANCH),
                  vec(D_BRANCH), vec(D_BRANCH), vec(D_BRANCH), vec(D_BRANCH), vec(D_BRANCH),
                  mat(D_BRANCH, D_BRANCH)],
        out_specs=pl.BlockSpec((C, D_BRANCH), lambda b, c: (b * nc + c, 0)),
        out_shape=jax.ShapeDtypeStruct((T, D_BRANCH), BF16),
        scratch_shapes=[pltpu.VMEM((N_HEADS, HEAD_DIM, HEAD_DIM), F32),
                        pltpu.VMEM((1, W_A), F32),
                        pltpu.VMEM((C, D_BRANCH), F32)],
        compiler_params=_params("parallel", "arbitrary"),
        name="rwkv",
    )(u, mu, w0, w2, a0, a2, k_k, k_a, r_k, gn_g, gn_b, hsel)


def _rope(x, cosw, sinw):
    W = x.shape[1]
    lane = lax.broadcasted_iota(I32, (1, W), 1) & (HEAD_DIM - 1)
    half = ROPE_DIM // 2
    partner = jnp.where(lane < half, pltpu.roll(x, W - half, 1), pltpu.roll(x, half, 1))
    return x * cosw + partner * sinw


def _dsa_prep_body(um_ref, ui_ref, cos_ref, sin_ref, qg_ref, kg_ref, hsel_ref,
                   qn_ref, kn_ref, vt_ref, qi_ref, ki_ref, wt_ref):
    DB = D_BRANCH
    cos1, sin1 = cos_ref[...], sin_ref[...]
    cos2, sin2 = jnp.concatenate([cos1] * 2, axis=1), jnp.concatenate([sin1] * 2, axis=1)
    cos4, sin4 = jnp.concatenate([cos2] * 2, axis=1), jnp.concatenate([sin2] * 2, axis=1)
    hsel = hsel_ref[...]

    def headnorm(x, gain):
        ms = _dot_sel(x * x, hsel) * (1.0 / HEAD_DIM)
        return x * lax.rsqrt(ms + NORM_EPS) * gain

    q = _rope(headnorm(um_ref[:, 0:DB], qg_ref[...]), cos4, sin4)
    k = _rope(headnorm(um_ref[:, DB:2 * DB], kg_ref[...]), cos4, sin4)
    qn_ref[...] = (q * (HEAD_DIM ** -0.5)).astype(BF16)
    kn_ref[...] = k.astype(BF16)
    vt_ref[...] = um_ref[:, 2 * DB:3 * DB].T.astype(BF16)
    qi_ref[...] = _rope(ui_ref[:, 0:IDX_HEADS * IDX_DIM], cos2, sin2).astype(BF16)
    kw = ui_ref[:, IDX_HEADS * IDX_DIM:]
    ki_ref[...] = _rope(kw, cos1, sin1)[:, 0:IDX_DIM].astype(BF16)
    wt_ref[...] = (kw * (IDX_HEADS ** -0.5 * IDX_DIM ** -0.5)).T


def _dsa_prep(u, B, S, cosf, sinf, qg, kg, hsel):
    T = u.shape[0]
    tm = min(512, S)
    ns = S // tm
    vec = lambda n: pl.BlockSpec((1, n), lambda i: (0, 0))
    row = lambda n: pl.BlockSpec((tm, n), lambda i: (i, 0))
    col = lambda n: pl.BlockSpec((None, n, tm), lambda i: (i // ns, 0, i % ns))
    return pl.pallas_call(
        _dsa_prep_body,
        grid=(T // tm,),
        in_specs=[pl.BlockSpec((tm, W_BM), lambda i: (i, OFF_BM // W_BM)),
                  pl.BlockSpec((tm, W_BI), lambda i: (i, OFF_BI // W_BI)),
                  row(LANES), row(LANES), vec(D_BRANCH), vec(D_BRANCH),
                  pl.BlockSpec((D_BRANCH, D_BRANCH), lambda i: (0, 0))],
        out_specs=[row(D_BRANCH), row(D_BRANCH), col(D_BRANCH), row(IDX_HEADS * IDX_DIM),
                   row(IDX_DIM), col(LANES)],
        out_shape=[jax.ShapeDtypeStruct((T, D_BRANCH), BF16),
                   jax.ShapeDtypeStruct((T, D_BRANCH), BF16),
                   jax.ShapeDtypeStruct((B, D_BRANCH, S), BF16),
                   jax.ShapeDtypeStruct((T, IDX_HEADS * IDX_DIM), BF16),
                   jax.ShapeDtypeStruct((T, IDX_DIM), BF16),
                   jax.ShapeDtypeStruct((B, LANES, S), F32)],
        compiler_params=_params("parallel"),
        name="dsa_prep",
    )(u, u, cosf, sinf, qg, kg, hsel)


def _dsa_body(qn_ref, qi_ref, wt_ref, g_ref, kn_ref, vt_ref, ki_ref, o_ref,
              key_ref, khi_ref, klo_ref, m_ref, l_ref, acc_ref, *, ksel, kt):
    QB = qn_ref.shape[0]
    S = kn_ref.shape[0]
    SUB = 8
    q0 = pl.program_id(1) * QB
    qlane = lax.broadcasted_iota(I32, (1, QB), 1)
    lim = (((q0 + qlane) >> CHUNK_SHIFT) + 1) << CHUNK_SHIFT
    n_t = (q0 + QB + kt - 1) // kt
    krow = lax.broadcasted_iota(I32, (kt, 1), 0)
    imin = jnp.int32(INT_MIN)
    qi = qi_ref[...]
    qi_h = [qi[:, h * IDX_DIM:(h + 1) * IDX_DIM] for h in range(IDX_HEADS)]
    w_h = [wt_ref[IDX_DIM + h:IDX_DIM + h + 1, :] for h in range(IDX_HEADS)]

    def fold(x, op):
        return op(x.reshape(kt // SUB, SUB, QB), axis=0)

    def score_tile(t, carry):
        off = pl.multiple_of(t * kt, kt)
        ki_t = ki_ref[pl.ds(off, kt), :]
        sc = jnp.zeros((kt, QB), F32)
        for h in range(IDX_HEADS):
            d = lax.dot_general(ki_t, qi_h[h], (((1,), (1,)), ((), ())),
                                preferred_element_type=F32)
            sc = sc + w_h[h] * jnp.maximum(d, 0.0)
        bits = pltpu.bitcast(sc, I32)
        key = bits ^ ((bits >> 31) & jnp.int32(0x7FFFFFFF))
        key = jnp.where(off + krow < lim, key, imin)
        key_ref[pl.ds(off, kt), :] = key
        khi_ref[pl.ds(off, kt), :] = (key >> 16).astype(I16)
        klo_ref[pl.ds(off, kt), :] = ((key & 0xFFFF) - HALF).astype(I16)
        return carry

    lax.fori_loop(0, n_t, score_tile, 0)

    def count16(ref, pred):
        def body(t, acc):
            off = pl.multiple_of(t * kt, kt)
            hit = jnp.where(pred(ref[pl.ds(off, kt), :]), jnp.int16(1), jnp.int16(0))
            parts = [hit[j * PACK:(j + 1) * PACK, :] for j in range(kt // PACK)]
            while len(parts) > 1:
                parts = [parts[j] + parts[j + 1] for j in range(0, len(parts), 2)]
            return acc + parts[0]

        acc = lax.fori_loop(0, n_t, body, jnp.zeros((PACK, QB), I16))
        return jnp.sum(acc.astype(I32), axis=0, keepdims=True)

    def bisect16(ref, target):
        def body(i, ulo):
            ucand = ulo + jnp.left_shift(jnp.int32(1), 15 - i)
            cand = (ucand - HALF).astype(I16)
            cnt = count16(ref, lambda x: x >= cand)
            return jnp.where(cnt >= target, ucand, ulo)

        return lax.fori_loop(0, 16, body, jnp.zeros((1, QB), I32))

    def count(pred):
        def body(t, acc):
            off = pl.multiple_of(t * kt, kt)
            hit = jnp.where(pred(key_ref[pl.ds(off, kt), :], off), 1, 0)
            return acc + fold(hit, jnp.sum)

        acc = lax.fori_loop(0, n_t, body, jnp.zeros((SUB, QB), I32))
        return jnp.sum(acc, axis=0, keepdims=True)

    uhi = bisect16(khi_ref, ksel)
    tau_hi = (uhi - HALF).astype(I16)
    k_rest = ksel - count16(khi_ref, lambda x: x > tau_hi)

    def mask_low(t, carry):
        off = pl.multiple_of(t * kt, kt)
        sl = pl.ds(off, kt)
        klo_ref[sl, :] = jnp.where(khi_ref[sl, :] == tau_hi, klo_ref[sl, :], jnp.int16(-HALF))
        return carry

    lax.fori_loop(0, n_t, mask_low, 0)
    ulo = bisect16(klo_ref, k_rest)
    tau = ((uhi - HALF) << 16) | ulo

    need = ksel - count(lambda key, off: key > tau)
    n_bits = int(np.log2(S))

    def tie_bisect(i, jlo):
        cand = jlo + jnp.left_shift(jnp.int32(1), n_bits - 1 - i)
        cnt = count(lambda key, off: jnp.where(key == tau, off + krow, S) < cand)
        return jnp.where(cnt < need, cand, jlo)

    has_ties = jnp.max(count(lambda key, off: key >= tau)) > ksel
    jmax = lax.cond(has_ties,
                    lambda: lax.fori_loop(0, n_bits, tie_bisect, jnp.zeros((1, QB), I32)),
                    lambda: jnp.full((1, QB), S, I32))

    m_ref[...] = jnp.full_like(m_ref, NEG_BIG)
    l_ref[...] = jnp.zeros_like(l_ref)
    acc_ref[...] = jnp.zeros_like(acc_ref)
    low_half = lax.broadcasted_iota(I32, (1, LANES), 1) < HEAD_DIM
    qn = qn_ref[...].astype(F32)
    q_pair = []
    for h in range(N_HEADS):
        pair = qn[:, (h // 2) * LANES:(h // 2 + 1) * LANES]
        own = low_half if h % 2 == 0 else jnp.logical_not(low_half)
        q_pair.append(jnp.where(own, pair, 0.0).astype(BF16))
    ONES_ROWS = 16
    ones = jnp.ones((ONES_ROWS, kt), BF16)

    def att_tile(t, carry):
        off = pl.multiple_of(t * kt, kt)
        key = key_ref[pl.ds(off, kt), :]
        tie_ok = jnp.where(key == tau, off + krow, S) <= jmax
        bias = jnp.where(key > tau, 0.0, jnp.where(tie_ok, 0.0, NEG_BIG))
        bias = jnp.where(key == imin, NEG_BIG, bias)
        heads = range(N_HEADS)
        pairs = [slice(pr * LANES, (pr + 1) * LANES) for pr in range(N_HEADS // 2)]
        k_p = [kn_ref[pl.ds(off, kt), ps] for ps in pairs]
        vo_p = [jnp.concatenate([vt_ref[ps, pl.ds(off, kt)], ones], axis=0) for ps in pairs]
        s = [lax.dot_general(k_p[h // 2], q_pair[h], (((1,), (1,)), ((), ())),
                             preferred_element_type=F32) + bias for h in heads]
        m_old = [m_ref[h] for h in heads]
        m_new = [jnp.maximum(m_old[h], jnp.max(fold(s[h], jnp.max), axis=0, keepdims=True))
                 for h in heads]
        p = [jnp.exp(s[h] - m_new[h][0:1, :]).astype(BF16) for h in heads]
        alpha = [jnp.exp(m_old[h] - m_new[h]) for h in heads]
        pvl = [jnp.dot(vo_p[h // 2], p[h], preferred_element_type=F32) for h in heads]
        for h in heads:
            l_ref[h] = alpha[h] * l_ref[h] + pvl[h][LANES:LANES + SUB, :]
            m_ref[h] = m_new[h]
            hs = slice(h * HEAD_DIM, (h + 1) * HEAD_DIM)
            own = slice((h % 2) * HEAD_DIM, (h % 2 + 1) * HEAD_DIM)
            acc_ref[hs, :] = alpha[h][0:1, :] * acc_ref[hs, :] + pvl[h][own, :]
        return carry

    lax.fori_loop(0, n_t, att_tile, 0)

    o_t = jnp.concatenate(
        [acc_ref[h * HEAD_DIM:(h + 1) * HEAD_DIM, :] / l_ref[h][0:1, :] for h in range(N_HEADS)],
        axis=0)
    g = g_ref[...]
    o_ref[...] = (o_t.T * (g * jax.nn.sigmoid(g))).astype(o_ref.dtype)


def _dsa(u, qn, kn, vt, qi, ki, wt, B, S, ksel):
    T = u.shape[0]
    QB = min(256, S)
    kt = min(512, S)
    nq = S // QB
    g_blk = (OFF_BM + 3 * D_BRANCH) // D_BRANCH
    qrow = lambda n: pl.BlockSpec((QB, n), lambda b, i: (b * nq + i, 0))
    full = lambda n: pl.BlockSpec((S, n), lambda b, i: (b, 0))
    return pl.pallas_call(
        functools.partial(_dsa_body, ksel=ksel, kt=kt),
        grid=(B, nq),
        in_specs=[qrow(D_BRANCH), qrow(IDX_HEADS * IDX_DIM),
                  pl.BlockSpec((None, LANES, QB), lambda b, i: (b, 0, i)),
                  pl.BlockSpec((QB, D_BRANCH), lambda b, i: (b * nq + i, g_blk)),
                  full(D_BRANCH),
                  pl.BlockSpec((None, D_BRANCH, S), lambda b, i: (b, 0, 0)),
                  full(IDX_DIM)],
        out_specs=qrow(D_BRANCH),
        out_shape=jax.ShapeDtypeStruct((T, D_BRANCH), BF16),
        scratch_shapes=[pltpu.VMEM((S, QB), I32),
                        pltpu.VMEM((S, QB), I16),
                        pltpu.VMEM((S, QB), I16),
                        pltpu.VMEM((N_HEADS, 8, QB), F32),
                        pltpu.VMEM((N_HEADS, 8, QB), F32),
                        pltpu.VMEM((D_BRANCH, QB), F32)],
        compiler_params=_params("parallel", "arbitrary"),
        name="dsa",
    )(qn, qi, wt, u, kn, vt, ki)


def _lru_body(u_ref, cw_ref, cb_ref, wri_ref, bri_ref, lam_ref, o_ref, xbuf_ref, h_ref):
    TS = u_ref.shape[0]
    DB = D_BRANCH
    PAD = 8

    @pl.when(pl.program_id(1) == 0)
    def _():
        xbuf_ref[0:PAD, :] = jnp.zeros((PAD, DB), F32)
        h_ref[...] = jnp.zeros_like(h_ref)

    x = u_ref[:, 0:DB]
    g = u_ref[:, DB:2 * DB]
    xbuf_ref[PAD:PAD + TS, :] = x
    xc = cb_ref[...]
    for j in range(CONV_WIDTH):
        xc = xc + cw_ref[j:j + 1, :] * xbuf_ref[pl.ds(PAD - (CONV_WIDTH - 1) + j, TS), :]
    xbuf_ref[0:PAD, :] = x[TS - PAD:TS, :]

    ri = jax.nn.sigmoid(_dot(xc, wri_ref[...]) + bri_ref[...])
    r, i = ri[:, 0:DB], ri[:, DB:2 * DB]
    log_a = (-LRU_C) * r * jax.nn.softplus(-lam_ref[...])
    a = jnp.exp(log_a)
    b = jnp.sqrt(jnp.tanh(-log_a) * (1.0 + a * a)) * (i * xc)

    row = lax.broadcasted_iota(I32, (TS, 1), 0)
    step = 1
    while step < TS:
        live = row >= step
        a_s = jnp.where(live, pltpu.roll(a, step, 0), 1.0)
        b_s = jnp.where(live, pltpu.roll(b, step, 0), 0.0)
        b = a * b_s + b
        a = a * a_s
        step *= 2
    h = b + a * h_ref[...]
    h_ref[...] = h[TS - 1:TS, :]
    o_ref[...] = (h * (g * jax.nn.sigmoid(g))).astype(o_ref.dtype)


def _lru(u, B, S, conv_w, conv_b, w_ri, b_ri, lam):
    T = u.shape[0]
    TS = min(256, S)
    ns = S // TS
    vec = lambda n: pl.BlockSpec((1, n), lambda b, s: (0, 0))
    return pl.pallas_call(
        _lru_body,
        grid=(B, ns),
        in_specs=[pl.BlockSpec((TS, W_C), lambda b, s: (b * ns + s, OFF_C // W_C)),
                  pl.BlockSpec((CONV_WIDTH, D_BRANCH), lambda b, s: (0, 0)),
                  vec(D_BRANCH),
                  pl.BlockSpec((D_BRANCH, 2 * D_BRANCH), lambda b, s: (0, 0)),
                  vec(2 * D_BRANCH), vec(D_BRANCH)],
        out_specs=pl.BlockSpec((TS, D_BRANCH), lambda b, s: (b * ns + s, 0)),
        out_shape=jax.ShapeDtypeStruct((T, D_BRANCH), BF16),
        scratch_shapes=[pltpu.VMEM((TS + 8, D_BRANCH), F32), pltpu.VMEM((1, D_BRANCH), F32)],
        compiler_params=_params("parallel", "arbitrary"),
        name="lru",
    )(u, conv_w, conv_b, w_ri, b_ri, lam)


def _merge_body(ya_ref, yb_ref, yc_ref, ug_ref, h_ref, p_ref, wb_ref, wo_ref, wp_ref, wpg_ref,
                o_ref):
    D = D_MODEL
    merged = None
    for n, y_ref in enumerate((ya_ref, yb_ref, yc_ref)):
        term = jax.nn.sigmoid(ug_ref[:, n * D:(n + 1) * D]) * _dot(y_ref[...], wb_ref[n])
        merged = term if merged is None else merged + term
    h1 = h_ref[...] + _dot(merged, wo_ref[...])
    gate = jax.nn.sigmoid(_dot(h1, wpg_ref[...]))
    o_ref[...] = h1 + gate * _dot(p_ref[...], wp_ref[...])


def _merge(ya, yb, yc, u, h, p, wb, wo, wp, wpg):
    T, D = h.shape
    tm = min(256, T)
    row = lambda n: pl.BlockSpec((tm, n), lambda i: (i, 0))
    return pl.pallas_call(
        _merge_body,
        grid=(T // tm,),
        in_specs=[row(D_BRANCH), row(D_BRANCH), row(D_BRANCH),
                  pl.BlockSpec((tm, W_G), lambda i: (i, OFF_G // W_G)),
                  row(D), row(PLE_DIM),
                  pl.BlockSpec((N_BRANCH, D_BRANCH, D), lambda i: (0, 0, 0)),
                  pl.BlockSpec((D, D), lambda i: (0, 0)),
                  pl.BlockSpec((PLE_DIM, D), lambda i: (0, 0)),
                  pl.BlockSpec((D, D), lambda i: (0, 0))],
        out_specs=row(D),
        out_shape=jax.ShapeDtypeStruct((T, D), F32),
        compiler_params=_params("parallel"),
        name="merge",
    )(ya, yb, yc, u, h, p, wb, wo, wp, wpg)


def _reorder_w_in(w):
    a0 = 0
    b0 = a0 + W_A
    c0 = b0 + 3 * D_BRANCH + IDX_HEADS * IDX_DIM + IDX_DIM + IDX_HEADS + D_BRANCH
    g0 = c0 + W_C
    bq = w[:, b0:b0 + 3 * D_BRANCH]
    bi0 = b0 + 3 * D_BRANCH
    n_idx = IDX_HEADS * IDX_DIM + IDX_DIM + IDX_HEADS
    bidx = w[:, bi0:bi0 + n_idx]
    bg = w[:, bi0 + n_idx:c0]
    pad = jnp.zeros((w.shape[0], W_BI - n_idx), w.dtype)
    out = jnp.concatenate([w[:, g0:g0 + W_G], w[:, c0:g0], bq, bg, bidx, pad, w[:, a0:b0]], axis=1)
    return _to_bf16(out)


def _cast_body(x_ref, o_ref):
    o_ref[...] = x_ref[...].astype(o_ref.dtype)


def _to_bf16(w):
    R, N = w.shape
    tn = 512
    return pl.pallas_call(
        _cast_body,
        grid=(N // tn,),
        in_specs=[pl.BlockSpec((R, tn), lambda j: (0, j))],
        out_specs=pl.BlockSpec((R, tn), lambda j: (0, j)),
        out_shape=jax.ShapeDtypeStruct((R, N), BF16),
        compiler_params=_params("parallel"),
        name="cast_bf16",
    )(w)


def _block_diag(w):
    H, n, _ = w.shape
    eye = jnp.eye(H, dtype=w.dtype)
    return (eye[:, None, :, None] * w[:, :, None, :]).reshape(H * n, H * n)


def _rope_tables(positions):
    half = ROPE_DIM // 2
    inv_freq = ROPE_THETA ** (-jnp.arange(0, ROPE_DIM, 2, dtype=F32) / ROPE_DIM)
    ang = positions.astype(F32).reshape(-1, 1) * inv_freq
    cos, sin = jnp.cos(ang), jnp.sin(ang)
    T = cos.shape[0]
    rest = HEAD_DIM - ROPE_DIM
    cos_h = jnp.concatenate([cos, cos, jnp.ones((T, rest), F32)], axis=1)
    sin_h = jnp.concatenate([-sin, sin, jnp.zeros((T, rest), F32)], axis=1)
    return jnp.tile(cos_h, (1, LANES // HEAD_DIM)), jnp.tile(sin_h, (1, LANES // HEAD_DIM))


def kernel(x, p, positions, norm_g, w_in, rwkv_mu, rwkv_w0, rwkv_w2, rwkv_a0, rwkv_a2, rwkv_k_k,
           rwkv_k_a, rwkv_r_k, rwkv_gn_g, rwkv_gn_b, dsa_q_g, dsa_k_g, lru_conv_w, lru_conv_b,
           lru_w_r, lru_b_r, lru_w_i, lru_b_i, lru_lambda, w_branch, w_out, w_ple, w_ple_gate):
    B, S, D = x.shape
    depth = w_in.shape[0]
    T = B * S
    ksel = min(TOPK_MAX, S // 4)
    cosf, sinf = _rope_tables(positions)
    head_id = np.arange(D_BRANCH) // HEAD_DIM
    hsel = jnp.asarray(head_id[:, None] == head_id[None, :], BF16)
    row = lambda v: v.reshape(1, -1)
    per_head = lambda v: jnp.tile(v, N_HEADS).reshape(1, -1)

    h = x.reshape(T, D)
    for i in range(depth):
        u = _inproj(h, row(norm_g[i]), _reorder_w_in(w_in[i]))
        ya = _rwkv(u, B, S, row(rwkv_mu[i]), row(rwkv_w0[i]), rwkv_w2[i], row(rwkv_a0[i]),
                   rwkv_a2[i], row(rwkv_k_k[i]), row(rwkv_k_a[i]), row(rwkv_r_k[i]),
                   row(rwkv_gn_g[i]), row(rwkv_gn_b[i]), hsel)
        qn, kn, vt, qi, ki, wt = _dsa_prep(u, B, S, cosf, sinf, per_head(dsa_q_g[i]),
                                           per_head(dsa_k_g[i]), hsel)
        yb = _dsa(u, qn, kn, vt, qi, ki, wt, B, S, ksel)
        w_ri = jnp.concatenate([_block_diag(lru_w_r[i]), _block_diag(lru_w_i[i])], axis=1)
        b_ri = jnp.concatenate([lru_b_r[i], lru_b_i[i]]).reshape(1, -1)
        yc = _lru(u, B, S, lru_conv_w[i], row(lru_conv_b[i]), w_ri.astype(BF16), b_ri,
                  row(lru_lambda[i]))
        h = _merge(ya, yb, yc, u, h, p[i].reshape(T, PLE_DIM), w_branch[i].astype(BF16),
                   w_out[i].astype(BF16), w_ple[i].astype(BF16), w_ple_gate[i].astype(BF16))
    return h.reshape(B, S, D)
```

```python
import functools

import numpy as np
import jax
import jax.numpy as jnp
from jax import lax
from jax.experimental import pallas as pl
from jax.experimental.pallas import tpu as pltpu

F32 = jnp.float32
BF16 = jnp.bfloat16
I32 = jnp.int32
I16 = jnp.int16
HALF = 1 << 15
PACK = 16

D_MODEL = 1024
D_BRANCH = 512
HEAD_DIM = 64
N_HEADS = 8
N_BRANCH = 3
PLE_DIM = 256
CHUNK = 64
CHUNK_SHIFT = 6
NORM_EPS = 1e-6
LORA = 64
RWKV_GN_EPS = 64e-5
IDX_HEADS = 4
IDX_DIM = 64
TOPK_MAX = 256
ROPE_THETA = 500000.0
ROPE_DIM = 16
CONV_WIDTH = 4
LRU_C = 8.0

LANES = 128
VMEM_LIMIT = 48 * 1024 * 1024

W_G = N_BRANCH * D_MODEL
W_C = 2 * D_BRANCH
W_BM = 4 * D_BRANCH
W_BI = 384
W_A = 4 * D_BRANCH + 2 * LORA
OFF_G, OFF_C, OFF_BM, OFF_BI, OFF_A = 0, 3072, 4096, 6144, 6528
D_IN_PAD = OFF_A + W_A

NEG_BIG = -1e30
INT_MIN = -2 ** 31


def _dot(a, b):
    return jnp.dot(a.astype(BF16), b.astype(BF16), preferred_element_type=F32)


def _dot_nt(a, b):
    return lax.dot_general(a.astype(BF16), b.astype(BF16), (((1,), (1,)), ((), ())),
                           preferred_element_type=F32)


SMALL_DOT_PIECES = 1


def _pieces(x):
    hi = x.astype(BF16)
    if SMALL_DOT_PIECES == 1:
        return [hi]
    return [hi, (x - hi.astype(F32)).astype(BF16)]


def _sdot_general(a, b, dims):
    pa, pb = _pieces(a), _pieces(b)
    out = None
    for i, x in enumerate(pa):
        for j, y in enumerate(pb):
            if i + j < SMALL_DOT_PIECES:
                t = lax.dot_general(x, y, (dims, ((), ())), preferred_element_type=F32)
                out = t if out is None else out + t
    return out


def _sdot(a, b):
    return _sdot_general(a, b, ((1,), (0,)))


def _sdot_nt(a, b):
    return _sdot_general(a, b, ((1,), (1,)))


def _dot_sel(x, sel_bf16):
    hi = x.astype(BF16)
    r1 = x - hi.astype(F32)
    mid = r1.astype(BF16)
    lo = (r1 - mid.astype(F32)).astype(BF16)
    out = jnp.dot(hi, sel_bf16, preferred_element_type=F32)
    out = out + jnp.dot(mid, sel_bf16, preferred_element_type=F32)
    return out + jnp.dot(lo, sel_bf16, preferred_element_type=F32)


def _params(*sem):
    return pltpu.CompilerParams(dimension_semantics=sem, vmem_limit_bytes=VMEM_LIMIT)


def _inproj_body(h_ref, g_ref, w_ref, o_ref, hn_ref):
    @pl.when(pl.program_id(1) == 0)
    def _():
        x = h_ref[...]
        ms = jnp.mean(x * x, axis=-1, keepdims=True)
        hn_ref[...] = (x * lax.rsqrt(ms + NORM_EPS) * g_ref[...]).astype(BF16)

    o_ref[...] = jnp.dot(hn_ref[...], w_ref[...], preferred_element_type=F32)


def _inproj(h, g, w):
    T, D = h.shape
    N = w.shape[1]
    tm, tn = min(2048, T), 512
    return pl.pallas_call(
        _inproj_body,
        grid=(T // tm, N // tn),
        in_specs=[pl.BlockSpec((tm, D), lambda i, j: (i, 0)),
                  pl.BlockSpec((1, D), lambda i, j: (0, 0)),
                  pl.BlockSpec((D, tn), lambda i, j: (0, j))],
        out_specs=pl.BlockSpec((tm, tn), lambda i, j: (i, j)),
        out_shape=jax.ShapeDtypeStruct((T, N), F32),
        scratch_shapes=[pltpu.VMEM((tm, D), BF16)],
        compiler_params=_params("parallel", "arbitrary"),
        name="inproj",
    )(h, g, w)


def _rwkv_body(ua_ref, mu_ref, w0_ref, w2_ref, a0_ref, a2_ref, kk_ref, ka_ref, rk_ref,
               gng_ref, gnb_ref, hsel_ref, o_ref, z_ref, prev_ref, obuf_ref):
    NB, C, _ = ua_ref.shape
    R = NB * C
    DB = D_BRANCH

    @pl.when(pl.program_id(0) == 0)
    def _():
        z_ref[...] = jnp.zeros_like(z_ref)
        prev_ref[...] = jnp.zeros_like(prev_ref)

    ua = ua_ref[...].reshape(R, ua_ref.shape[2])
    row = lax.broadcasted_iota(I32, (R, 1), 0)
    shifted = pltpu.roll(ua, 1, 0)
    for bi in range(NB):
        shifted = jnp.where(row == bi * C, prev_ref[bi], shifted)
    for bi in range(NB):
        prev_ref[bi] = ua[(bi + 1) * C - 1:(bi + 1) * C, :]
    xs = ua + (shifted - ua) * mu_ref[...]

    r = xs[:, 0:DB]
    k = xs[:, DB:2 * DB]
    v = xs[:, 2 * DB:3 * DB]
    wl = xs[:, 3 * DB:3 * DB + LORA]
    al = xs[:, 3 * DB + LORA:3 * DB + 2 * LORA]
    g = xs[:, 3 * DB + 2 * LORA:]
    hsel = hsel_ref[...]

    w_pre = w0_ref[...] + _dot(jnp.tanh(wl), w2_ref[...])
    lw = -jnp.exp(-jax.nn.softplus(-w_pre) - 0.5)
    a = jax.nn.sigmoid(a0_ref[...] + _dot(al, a2_ref[...]))
    kk = k * kk_ref[...]
    k_mod = k * (1.0 + (a - 1.0) * ka_ref[...])
    sums = _dot_sel(jnp.concatenate([kk * kk, r * k_mod * rk_ref[...]], axis=0), hsel)
    kk = kk / jnp.maximum(jnp.sqrt(sums[0:R, :]), 1e-12)
    bonus = sums[R:2 * R, :] * v
    b = kk * a

    ri = lax.broadcasted_iota(I32, (R, R), 0)
    rj = lax.broadcasted_iota(I32, (R, R), 1)
    same_chunk = (ri >> CHUNK_SHIFT) == (rj >> CHUNK_SHIFT)
    cum_sel = jnp.where(same_chunk, jnp.where(ri >= rj, 1.0, 0.0), 0.0).astype(BF16)
    cl = _dot_sel_lhs(cum_sel, lw)
    cl_end = jnp.concatenate(
        [jnp.broadcast_to(cl[(bi + 1) * C - 1:(bi + 1) * C, :], (C, DB)) for bi in range(NB)], axis=0)
    g_incl = jnp.exp(cl)
    g_inv = jnp.exp(-cl)
    d_end = jnp.exp(cl_end - cl)
    kt_all = kk * jnp.exp(cl - lw)
    rt_all = r * g_incl
    bs_all = b * g_inv
    ks_all = k_mod * g_inv
    bh_all = b * d_end
    kh_all = k_mod * d_end

    ci = lax.broadcasted_iota(I32, (C, C), 0)
    cj = lax.broadcasted_iota(I32, (C, C), 1)
    incl = ci >= cj
    strict = ci > cj
    eye = ci == cj
    ident = jnp.where(eye, 1.0, 0.0)
    n_sq = int(np.log2(C)) - 1
    rows = [slice(bi * C, (bi + 1) * C) for bi in range(NB)]
    bh_t = [bh_all[rb, :].T for rb in rows]
    kh_t = [kh_all[rb, :].T for rb in rows]
    units = [(bi, h) for bi in range(NB) for h in range(N_HEADS)]
    un = range(len(units))
    hsl = [slice(h * HEAD_DIM, (h + 1) * HEAD_DIM) for _, h in units]
    rsl = [rows[bi] for bi, _ in units]
    kt = [kt_all[rsl[u], hsl[u]] for u in un]
    rt = [rt_all[rsl[u], hsl[u]] for u in un]
    vh = [v[rsl[u], hsl[u]] for u in un]
    gram = [_sdot_nt(jnp.concatenate([kt[u], rt[u]], axis=0),
                     jnp.concatenate([bs_all[rsl[u], hsl[u]], ks_all[rsl[u], hsl[u]]], axis=0))
            for u in un]
    b_m = [jnp.where(strict, gram[u][0:C, C:2 * C], 0.0) for u in un]
    ab = [jnp.where(incl, gram[u][C:2 * C, 0:C], 0.0) for u in un]
    bk = [jnp.where(incl, gram[u][C:2 * C, C:2 * C], 0.0) for u in un]
    pw = [jnp.where(strict, -gram[u][0:C, 0:C], 0.0) for u in un]
    tinv = [ident + pw[u] for u in un]
    bmv = [_sdot(b_m[u], vh[u]) for u in un]
    nv = [_sdot(jnp.concatenate([kh_t[units[u][0]][hsl[u], :], bk[u]], axis=0), vh[u]) for u in un]
    for _ in range(n_sq):
        pw = [_sdot(pw[u], pw[u]) for u in un]
        tinv = [_sdot(tinv[u], ident + pw[u]) for u in un]
    wu = [-_sdot(tinv[u], jnp.concatenate([kt[u], bmv[u]], axis=1)) for u in un]
    mq = [_sdot(jnp.concatenate([bh_t[units[u][0]][hsl[u], :], ab[u]], axis=0), wu[u]) for u in un]
    lhs = []
    for u in un:
        g_end = g_incl[rsl[u].stop - 1:rsl[u].stop, hsl[u]]
        m_m = jnp.where(eye, g_end, 0.0) + mq[u][0:HEAD_DIM, 0:HEAD_DIM]
        qe = rt[u] + mq[u][HEAD_DIM:, 0:HEAD_DIM]
        lhs.append(jnp.concatenate([qe, m_m], axis=0))
    oz = [_sdot(lhs[u], z_ref[u]) for u in un]
    for u in un:
        n_m = mq[u][0:HEAD_DIM, HEAD_DIM:] + nv[u][0:HEAD_DIM, :]
        o0 = mq[u][HEAD_DIM:, HEAD_DIM:] + nv[u][HEAD_DIM:, :]
        obuf_ref[rsl[u], hsl[u]] = oz[u][0:C, :] + o0
        z_ref[u] = oz[u][C:, :] + n_m

    o = obuf_ref[...]
    inv_n = 1.0 / HEAD_DIM
    mean = _dot_sel(o, hsel) * inv_n
    oc = o - mean
    var = _dot_sel(oc * oc, hsel) * inv_n
    o = oc * lax.rsqrt(var + RWKV_GN_EPS) * gng_ref[...] + gnb_ref[...]
    y = (o + bonus) * (g * jax.nn.sigmoid(g))
    o_ref[...] = y.reshape(NB, C, DB).astype(o_ref.dtype)


def _dot_sel_lhs(sel_bf16, x):
    hi = x.astype(BF16)
    r1 = x - hi.astype(F32)
    mid = r1.astype(BF16)
    lo = (r1 - mid.astype(F32)).astype(BF16)
    out = jnp.dot(sel_bf16, hi, preferred_element_type=F32)
    out = out + jnp.dot(sel_bf16, mid, preferred_element_type=F32)
    return out + jnp.dot(sel_bf16, lo, preferred_element_type=F32)


def _rwkv(u, B, S, mu, w0, w2, a0, a2, k_k, k_a, r_k, gn_g, gn_b, hsel):
    T = u.shape[0]
    C = CHUNK
    nc = S // C
    a_blk = OFF_A // W_A
    vec = lambda n: pl.BlockSpec((1, n), lambda c: (0, 0))
    mat = lambda m, n: pl.BlockSpec((m, n), lambda c: (0, 0))
    out = pl.pallas_call(
        _rwkv_body,
        grid=(nc,),
        in_specs=[pl.BlockSpec((B, C, W_A), lambda c: (0, c, a_blk)),
                  vec(W_A), vec(D_BRANCH), mat(LORA, D_BRANCH), vec(D_BRANCH), mat(LORA, D_BRANCH),
                  vec(D_BRANCH), vec(D_BRANCH), vec(D_BRANCH), vec(D_BRANCH), vec(D_BRANCH),
                  mat(D_BRANCH, D_BRANCH)],
        out_specs=pl.BlockSpec((B, C, D_BRANCH), lambda c: (0, c, 0)),
        out_shape=jax.ShapeDtypeStruct((B, S, D_BRANCH), BF16),
        scratch_shapes=[pltpu.VMEM((B * N_HEADS, HEAD_DIM, HEAD_DIM), F32),
                        pltpu.VMEM((B, 1, W_A), F32),
                        pltpu.VMEM((B * C, D_BRANCH), F32)],
        compiler_params=_params("arbitrary"),
        name="rwkv",
    )(u.reshape(B, S, u.shape[1]), mu, w0, w2, a0, a2, k_k, k_a, r_k, gn_g, gn_b, hsel)
    return out.reshape(T, D_BRANCH)


def _rope(x, cosw, sinw):
    W = x.shape[1]
    lane = lax.broadcasted_iota(I32, (1, W), 1) & (HEAD_DIM - 1)
    half = ROPE_DIM // 2
    partner = jnp.where(lane < half, pltpu.roll(x, W - half, 1), pltpu.roll(x, half, 1))
    return x * cosw + partner * sinw


def _dsa_prep_body(um_ref, ui_ref, cos_ref, sin_ref, qg_ref, kg_ref, hsel_ref,
                   qn_ref, kn_ref, vt_ref, qi_ref, ki_ref, wt_ref):
    DB = D_BRANCH
    cos1, sin1 = cos_ref[...], sin_ref[...]
    cos2, sin2 = jnp.concatenate([cos1] * 2, axis=1), jnp.concatenate([sin1] * 2, axis=1)
    cos4, sin4 = jnp.concatenate([cos2] * 2, axis=1), jnp.concatenate([sin2] * 2, axis=1)
    hsel = hsel_ref[...]

    def headnorm(x, gain):
        ms = _dot_sel(x * x, hsel) * (1.0 / HEAD_DIM)
        return x * lax.rsqrt(ms + NORM_EPS) * gain

    q = _rope(headnorm(um_ref[:, 0:DB], qg_ref[...]), cos4, sin4)
    k = _rope(headnorm(um_ref[:, DB:2 * DB], kg_ref[...]), cos4, sin4)
    qn_ref[...] = (q * (HEAD_DIM ** -0.5)).astype(BF16)
    kn_ref[...] = k.astype(BF16)
    vt_ref[...] = um_ref[:, 2 * DB:3 * DB].T.astype(BF16)
    qi_ref[...] = _rope(ui_ref[:, 0:IDX_HEADS * IDX_DIM], cos2, sin2).astype(BF16)
    kw = ui_ref[:, IDX_HEADS * IDX_DIM:]
    ki_ref[...] = _rope(kw, cos1, sin1)[:, 0:IDX_DIM].astype(BF16)
    wt_ref[...] = (kw * (IDX_HEADS ** -0.5 * IDX_DIM ** -0.5)).T


def _dsa_prep(u, B, S, cosf, sinf, qg, kg, hsel):
    T = u.shape[0]
    tm = min(512, S)
    ns = S // tm
    vec = lambda n: pl.BlockSpec((1, n), lambda i: (0, 0))
    row = lambda n: pl.BlockSpec((tm, n), lambda i: (i, 0))
    col = lambda n: pl.BlockSpec((None, n, tm), lambda i: (i // ns, 0, i % ns))
    return pl.pallas_call(
        _dsa_prep_body,
        grid=(T // tm,),
        in_specs=[pl.BlockSpec((tm, W_BM), lambda i: (i, OFF_BM // W_BM)),
                  pl.BlockSpec((tm, W_BI), lambda i: (i, OFF_BI // W_BI)),
                  row(LANES), row(LANES), vec(D_BRANCH), vec(D_BRANCH),
                  pl.BlockSpec((D_BRANCH, D_BRANCH), lambda i: (0, 0))],
        out_specs=[row(D_BRANCH), row(D_BRANCH), col(D_BRANCH), row(IDX_HEADS * IDX_DIM),
                   row(IDX_DIM), col(LANES)],
        out_shape=[jax.ShapeDtypeStruct((T, D_BRANCH), BF16),
                   jax.ShapeDtypeStruct((T, D_BRANCH), BF16),
                   jax.ShapeDtypeStruct((B, D_BRANCH, S), BF16),
                   jax.ShapeDtypeStruct((T, IDX_HEADS * IDX_DIM), BF16),
                   jax.ShapeDtypeStruct((T, IDX_DIM), BF16),
                   jax.ShapeDtypeStruct((B, LANES, S), F32)],
        compiler_params=_params("parallel"),
        name="dsa_prep",
    )(u, u, cosf, sinf, qg, kg, hsel)


def _dsa_body(qn_ref, qi_ref, wt_ref, g_ref, kn_ref, vt_ref, ki_ref, o_ref,
              key_ref, khi_ref, klo_ref, m_ref, l_ref, acc_ref, *, ksel, kt):
    QB = qn_ref.shape[0]
    S = kn_ref.shape[0]
    SUB = 8
    q0 = pl.program_id(1) * QB
    qlane = lax.broadcasted_iota(I32, (1, QB), 1)
    lim = (((q0 + qlane) >> CHUNK_SHIFT) + 1) << CHUNK_SHIFT
    n_t = (q0 + QB + kt - 1) // kt
    krow = lax.broadcasted_iota(I32, (kt, 1), 0)
    imin = jnp.int32(INT_MIN)
    qi = qi_ref[...]
    qi_h = [qi[:, h * IDX_DIM:(h + 1) * IDX_DIM] for h in range(IDX_HEADS)]
    w_h = [wt_ref[IDX_DIM + h:IDX_DIM + h + 1, :] for h in range(IDX_HEADS)]

    def fold(x, op):
        return op(x.reshape(kt // SUB, SUB, QB), axis=0)

    def score_tile(t, carry):
        off = pl.multiple_of(t * kt, kt)
        ki_t = ki_ref[pl.ds(off, kt), :]
        sc = jnp.zeros((kt, QB), F32)
        for h in range(IDX_HEADS):
            d = lax.dot_general(ki_t, qi_h[h], (((1,), (1,)), ((), ())),
                                preferred_element_type=F32)
            sc = sc + w_h[h] * jnp.maximum(d, 0.0)
        bits = pltpu.bitcast(sc, I32)
        key = bits ^ ((bits >> 31) & jnp.int32(0x7FFFFFFF))
        key = jnp.where(off + krow < lim, key, imin)
        key_ref[pl.ds(off, kt), :] = key
        khi_ref[pl.ds(off, kt), :] = (key >> 16).astype(I16)
        klo_ref[pl.ds(off, kt), :] = ((key & 0xFFFF) - HALF).astype(I16)
        return carry

    lax.fori_loop(0, n_t, score_tile, 0)

    def count16(ref, pred):
        def body(t, acc):
            off = pl.multiple_of(t * kt, kt)
            hit = jnp.where(pred(ref[pl.ds(off, kt), :]), jnp.int16(1), jnp.int16(0))
            parts = [hit[j * PACK:(j + 1) * PACK, :] for j in range(kt // PACK)]
            while len(parts) > 1:
                parts = [parts[j] + parts[j + 1] for j in range(0, len(parts), 2)]
            return acc + parts[0]

        acc = lax.fori_loop(0, n_t, body, jnp.zeros((PACK, QB), I16))
        return jnp.sum(acc.astype(I32), axis=0, keepdims=True)

    def bisect16(ref, target):
        def body(i, ulo):
            ucand = ulo + jnp.left_shift(jnp.int32(1), 15 - i)
            cand = (ucand - HALF).astype(I16)
            cnt = count16(ref, lambda x: x >= cand)
            return jnp.where(cnt >= target, ucand, ulo)

        return lax.fori_loop(0, 16, body, jnp.zeros((1, QB), I32))

    def count(pred):
        def body(t, acc):
            off = pl.multiple_of(t * kt, kt)
            hit = jnp.where(pred(key_ref[pl.ds(off, kt), :], off), 1, 0)
            return acc + fold(hit, jnp.sum)

        acc = lax.fori_loop(0, n_t, body, jnp.zeros((SUB, QB), I32))
        return jnp.sum(acc, axis=0, keepdims=True)

    uhi = bisect16(khi_ref, ksel)
    tau_hi = (uhi - HALF).astype(I16)
    k_rest = ksel - count16(khi_ref, lambda x: x > tau_hi)

    def mask_low(t, carry):
        off = pl.multiple_of(t * kt, kt)
        sl = pl.ds(off, kt)
        klo_ref[sl, :] = jnp.where(khi_ref[sl, :] == tau_hi, klo_ref[sl, :], jnp.int16(-HALF))
        return carry

    lax.fori_loop(0, n_t, mask_low, 0)
    ulo = bisect16(klo_ref, k_rest)
    tau = ((uhi - HALF) << 16) | ulo

    need = ksel - count(lambda key, off: key > tau)
    n_bits = int(np.log2(S))

    def tie_bisect(i, jlo):
        cand = jlo + jnp.left_shift(jnp.int32(1), n_bits - 1 - i)
        cnt = count(lambda key, off: jnp.where(key == tau, off + krow, S) < cand)
        return jnp.where(cnt < need, cand, jlo)

    has_ties = jnp.max(count(lambda key, off: key >= tau)) > ksel
    jmax = lax.cond(has_ties,
                    lambda: lax.fori_loop(0, n_bits, tie_bisect, jnp.zeros((1, QB), I32)),
                    lambda: jnp.full((1, QB), S, I32))

    m_ref[...] = jnp.full_like(m_ref, NEG_BIG)
    l_ref[...] = jnp.zeros_like(l_ref)
    acc_ref[...] = jnp.zeros_like(acc_ref)
    low_half = lax.broadcasted_iota(I32, (1, LANES), 1) < HEAD_DIM
    qn = qn_ref[...].astype(F32)
    q_pair = []
    for h in range(N_HEADS):
        pair = qn[:, (h // 2) * LANES:(h // 2 + 1) * LANES]
        own = low_half if h % 2 == 0 else jnp.logical_not(low_half)
        q_pair.append(jnp.where(own, pair, 0.0).astype(BF16))
    ONES_ROWS = 16
    ones = jnp.ones((ONES_ROWS, kt), BF16)

    def att_tile(t, carry):
        off = pl.multiple_of(t * kt, kt)
        key = key_ref[pl.ds(off, kt), :]
        tie_ok = jnp.where(key == tau, off + krow, S) <= jmax
        bias = jnp.where(key > tau, 0.0, jnp.where(tie_ok, 0.0, NEG_BIG))
        bias = jnp.where(key == imin, NEG_BIG, bias)
        heads = range(N_HEADS)
        pairs = [slice(pr * LANES, (pr + 1) * LANES) for pr in range(N_HEADS // 2)]
        k_p = [kn_ref[pl.ds(off, kt), ps] for ps in pairs]
        vo_p = [jnp.concatenate([vt_ref[ps, pl.ds(off, kt)], ones], axis=0) for ps in pairs]
        s = [lax.dot_general(k_p[h // 2], q_pair[h], (((1,), (1,)), ((), ())),
                             preferred_element_type=F32) + bias for h in heads]
        m_old = [m_ref[h] for h in heads]
        m_new = [jnp.maximum(m_old[h], jnp.max(fold(s[h], jnp.max), axis=0, keepdims=True))
                 for h in heads]
        p = [jnp.exp(s[h] - m_new[h][0:1, :]).astype(BF16) for h in heads]
        alpha = [jnp.exp(m_old[h] - m_new[h]) for h in heads]
        pvl = [jnp.dot(vo_p[h // 2], p[h], preferred_element_type=F32) for h in heads]
        for h in heads:
            l_ref[h] = alpha[h] * l_ref[h] + pvl[h][LANES:LANES + SUB, :]
            m_ref[h] = m_new[h]
            hs = slice(h * HEAD_DIM, (h + 1) * HEAD_DIM)
            own = slice((h % 2) * HEAD_DIM, (h % 2 + 1) * HEAD_DIM)
            acc_ref[hs, :] = alpha[h][0:1, :] * acc_ref[hs, :] + pvl[h][own, :]
        return carry

    lax.fori_loop(0, n_t, att_tile, 0)

    o_t = jnp.concatenate(
        [acc_ref[h * HEAD_DIM:(h + 1) * HEAD_DIM, :] / l_ref[h][0:1, :] for h in range(N_HEADS)],
        axis=0)
    g = g_ref[...]
    o_ref[...] = (o_t.T * (g * jax.nn.sigmoid(g))).astype(o_ref.dtype)


def _dsa(u, qn, kn, vt, qi, ki, wt, B, S, ksel):
    T = u.shape[0]
    QB = min(256, S)
    kt = min(512, S)
    nq = S // QB
    g_blk = (OFF_BM + 3 * D_BRANCH) // D_BRANCH
    qrow = lambda n: pl.BlockSpec((QB, n), lambda b, i: (b * nq + i, 0))
    full = lambda n: pl.BlockSpec((S, n), lambda b, i: (b, 0))
    return pl.pallas_call(
        functools.partial(_dsa_body, ksel=ksel, kt=kt),
        grid=(B, nq),
        in_specs=[qrow(D_BRANCH), qrow(IDX_HEADS * IDX_DIM),
                  pl.BlockSpec((None, LANES, QB), lambda b, i: (b, 0, i)),
                  pl.BlockSpec((QB, D_BRANCH), lambda b, i: (b * nq + i, g_blk)),
                  full(D_BRANCH),
                  pl.BlockSpec((None, D_BRANCH, S), lambda b, i: (b, 0, 0)),
                  full(IDX_DIM)],
        out_specs=qrow(D_BRANCH),
        out_shape=jax.ShapeDtypeStruct((T, D_BRANCH), BF16),
        scratch_shapes=[pltpu.VMEM((S, QB), I32),
                        pltpu.VMEM((S, QB), I16),
                        pltpu.VMEM((S, QB), I16),
                        pltpu.VMEM((N_HEADS, 8, QB), F32),
                        pltpu.VMEM((N_HEADS, 8, QB), F32),
                        pltpu.VMEM((D_BRANCH, QB), F32)],
        compiler_params=_params("parallel", "arbitrary"),
        name="dsa",
    )(qn, qi, wt, u, kn, vt, ki)


def _lru_body(u_ref, cw_ref, cb_ref, wri_ref, bri_ref, lam_ref, o_ref, xbuf_ref, h_ref):
    TS = u_ref.shape[0]
    DB = D_BRANCH
    PAD = 8

    @pl.when(pl.program_id(1) == 0)
    def _():
        xbuf_ref[0:PAD, :] = jnp.zeros((PAD, DB), F32)
        h_ref[...] = jnp.zeros_like(h_ref)

    x = u_ref[:, 0:DB]
    g = u_ref[:, DB:2 * DB]
    xbuf_ref[PAD:PAD + TS, :] = x
    xc = cb_ref[...]
    for j in range(CONV_WIDTH):
        xc = xc + cw_ref[j:j + 1, :] * xbuf_ref[pl.ds(PAD - (CONV_WIDTH - 1) + j, TS), :]
    xbuf_ref[0:PAD, :] = x[TS - PAD:TS, :]

    ri = jax.nn.sigmoid(_dot(xc, wri_ref[...]) + bri_ref[...])
    r, i = ri[:, 0:DB], ri[:, DB:2 * DB]
    log_a = (-LRU_C) * r * jax.nn.softplus(-lam_ref[...])
    a = jnp.exp(log_a)
    b = jnp.sqrt(jnp.tanh(-log_a) * (1.0 + a * a)) * (i * xc)

    row = lax.broadcasted_iota(I32, (TS, 1), 0)
    step = 1
    while step < TS:
        live = row >= step
        a_s = jnp.where(live, pltpu.roll(a, step, 0), 1.0)
        b_s = jnp.where(live, pltpu.roll(b, step, 0), 0.0)
        b = a * b_s + b
        a = a * a_s
        step *= 2
    h = b + a * h_ref[...]
    h_ref[...] = h[TS - 1:TS, :]
    o_ref[...] = (h * (g * jax.nn.sigmoid(g))).astype(o_ref.dtype)


def _lru(u, B, S, conv_w, conv_b, w_ri, b_ri, lam):
    T = u.shape[0]
    TS = min(256, S)
    ns = S // TS
    vec = lambda n: pl.BlockSpec((1, n), lambda b, s: (0, 0))
    return pl.pallas_call(
        _lru_body,
        grid=(B, ns),
        in_specs=[pl.BlockSpec((TS, W_C), lambda b, s: (b * ns + s, OFF_C // W_C)),
                  pl.BlockSpec((CONV_WIDTH, D_BRANCH), lambda b, s: (0, 0)),
                  vec(D_BRANCH),
                  pl.BlockSpec((D_BRANCH, 2 * D_BRANCH), lambda b, s: (0, 0)),
                  vec(2 * D_BRANCH), vec(D_BRANCH)],
        out_specs=pl.BlockSpec((TS, D_BRANCH), lambda b, s: (b * ns + s, 0)),
        out_shape=jax.ShapeDtypeStruct((T, D_BRANCH), BF16),
        scratch_shapes=[pltpu.VMEM((TS + 8, D_BRANCH), F32), pltpu.VMEM((1, D_BRANCH), F32)],
        compiler_params=_params("parallel", "arbitrary"),
        name="lru",
    )(u, conv_w, conv_b, w_ri, b_ri, lam)


def _merge_body(ya_ref, yb_ref, yc_ref, ug_ref, h_ref, p_ref, wb_ref, wo_ref, wp_ref, wpg_ref,
                o_ref):
    D = D_MODEL
    merged = None
    for n, y_ref in enumerate((ya_ref, yb_ref, yc_ref)):
        term = jax.nn.sigmoid(ug_ref[:, n * D:(n + 1) * D]) * _dot(y_ref[...], wb_ref[n])
        merged = term if merged is None else merged + term
    h1 = h_ref[...] + _dot(merged, wo_ref[...])
    gate = jax.nn.sigmoid(_dot(h1, wpg_ref[...]))
    o_ref[...] = h1 + gate * _dot(p_ref[...], wp_ref[...])


def _merge(ya, yb, yc, u, h, p, wb, wo, wp, wpg):
    T, D = h.shape
    tm = min(256, T)
    row = lambda n: pl.BlockSpec((tm, n), lambda i: (i, 0))
    return pl.pallas_call(
        _merge_body,
        grid=(T // tm,),
        in_specs=[row(D_BRANCH), row(D_BRANCH), row(D_BRANCH),
                  pl.BlockSpec((tm, W_G), lambda i: (i, OFF_G // W_G)),
                  row(D), row(PLE_DIM),
                  pl.BlockSpec((N_BRANCH, D_BRANCH, D), lambda i: (0, 0, 0)),
                  pl.BlockSpec((D, D), lambda i: (0, 0)),
                  pl.BlockSpec((PLE_DIM, D), lambda i: (0, 0)),
                  pl.BlockSpec((D, D), lambda i: (0, 0))],
        out_specs=row(D),
        out_shape=jax.ShapeDtypeStruct((T, D), F32),
        compiler_params=_params("parallel"),
        name="merge",
    )(ya, yb, yc, u, h, p, wb, wo, wp, wpg)


def _reorder_w_in(w):
    a0 = 0
    b0 = a0 + W_A
    c0 = b0 + 3 * D_BRANCH + IDX_HEADS * IDX_DIM + IDX_DIM + IDX_HEADS + D_BRANCH
    g0 = c0 + W_C
    bq = w[:, b0:b0 + 3 * D_BRANCH]
    bi0 = b0 + 3 * D_BRANCH
    n_idx = IDX_HEADS * IDX_DIM + IDX_DIM + IDX_HEADS
    bidx = w[:, bi0:bi0 + n_idx]
    bg = w[:, bi0 + n_idx:c0]
    pad = jnp.zeros((w.shape[0], W_BI - n_idx), w.dtype)
    out = jnp.concatenate([w[:, g0:g0 + W_G], w[:, c0:g0], bq, bg, bidx, pad, w[:, a0:b0]], axis=1)
    return _to_bf16(out)


def _cast_body(x_ref, o_ref):
    o_ref[...] = x_ref[...].astype(o_ref.dtype)


def _to_bf16(w):
    R, N = w.shape
    tn = 512
    return pl.pallas_call(
        _cast_body,
        grid=(N // tn,),
        in_specs=[pl.BlockSpec((R, tn), lambda j: (0, j))],
        out_specs=pl.BlockSpec((R, tn), lambda j: (0, j)),
        out_shape=jax.ShapeDtypeStruct((R, N), BF16),
        compiler_params=_params("parallel"),
        name="cast_bf16",
    )(w)


def _block_diag(w):
    H, n, _ = w.shape
    eye = jnp.eye(H, dtype=w.dtype)
    return (eye[:, None, :, None] * w[:, :, None, :]).reshape(H * n, H * n)


def _rope_tables(positions):
    half = ROPE_DIM // 2
    inv_freq = ROPE_THETA ** (-jnp.arange(0, ROPE_DIM, 2, dtype=F32) / ROPE_DIM)
    ang = positions.astype(F32).reshape(-1, 1) * inv_freq
    cos, sin = jnp.cos(ang), jnp.sin(ang)
    T = cos.shape[0]
    rest = HEAD_DIM - ROPE_DIM
    cos_h = jnp.concatenate([cos, cos, jnp.ones((T, rest), F32)], axis=1)
    sin_h = jnp.concatenate([-sin, sin, jnp.zeros((T, rest), F32)], axis=1)
    return jnp.tile(cos_h, (1, LANES // HEAD_DIM)), jnp.tile(sin_h, (1, LANES // HEAD_DIM))


def kernel(x, p, positions, norm_g, w_in, rwkv_mu, rwkv_w0, rwkv_w2, rwkv_a0, rwkv_a2, rwkv_k_k,
           rwkv_k_a, rwkv_r_k, rwkv_gn_g, rwkv_gn_b, dsa_q_g, dsa_k_g, lru_conv_w, lru_conv_b,
           lru_w_r, lru_b_r, lru_w_i, lru_b_i, lru_lambda, w_branch, w_out, w_ple, w_ple_gate):
    B, S, D = x.shape
    depth = w_in.shape[0]
    T = B * S
    ksel = min(TOPK_MAX, S // 4)
    cosf, sinf = _rope_tables(positions)
    head_id = np.arange(D_BRANCH) // HEAD_DIM
    hsel = jnp.asarray(head_id[:, None] == head_id[None, :], BF16)
    row = lambda v: v.reshape(1, -1)
    per_head = lambda v: jnp.tile(v, N_HEADS).reshape(1, -1)

    h = x.reshape(T, D)
    for i in range(depth):
        u = _inproj(h, row(norm_g[i]), _reorder_w_in(w_in[i]))
        ya = _rwkv(u, B, S, row(rwkv_mu[i]), row(rwkv_w0[i]), rwkv_w2[i], row(rwkv_a0[i]),
                   rwkv_a2[i], row(rwkv_k_k[i]), row(rwkv_k_a[i]), row(rwkv_r_k[i]),
                   row(rwkv_gn_g[i]), row(rwkv_gn_b[i]), hsel)
        qn, kn, vt, qi, ki, wt = _dsa_prep(u, B, S, cosf, sinf, per_head(dsa_q_g[i]),
                                           per_head(dsa_k_g[i]), hsel)
        yb = _dsa(u, qn, kn, vt, qi, ki, wt, B, S, ksel)
        w_ri = jnp.concatenate([_block_diag(lru_w_r[i]), _block_diag(lru_w_i[i])], axis=1)
        b_ri = jnp.concatenate([lru_b_r[i], lru_b_i[i]]).reshape(1, -1)
        yc = _lru(u, B, S, lru_conv_w[i], row(lru_conv_b[i]), w_ri.astype(BF16), b_ri,
                  row(lru_lambda[i]))
        h = _merge(ya, yb, yc, u, h, p[i].reshape(T, PLE_DIM), w_branch[i].astype(BF16),
                   w_out[i].astype(BF16), w_ple[i].astype(BF16), w_ple_gate[i].astype(BF16))
    return h.reshape(B, S, D)
```

```python
import functools

import numpy as np
import jax
import jax.numpy as jnp
from jax import lax
from jax.experimental import pallas as pl
from jax.experimental.pallas import tpu as pltpu

F32 = jnp.float32
BF16 = jnp.bfloat16
I32 = jnp.int32
I16 = jnp.int16
HALF = 1 << 15
PACK = 16
LOG2E = 1.4426950408889634
HEAD_GROUP = 8

D_MODEL = 1024
D_BRANCH = 512
HEAD_DIM = 64
N_HEADS = 8
N_BRANCH = 3
PLE_DIM = 256
CHUNK = 64
CHUNK_SHIFT = 6
NORM_EPS = 1e-6
LORA = 64
RWKV_GN_EPS = 64e-5
IDX_HEADS = 4
IDX_DIM = 64
TOPK_MAX = 256
ROPE_THETA = 500000.0
ROPE_DIM = 16
CONV_WIDTH = 4
LRU_C = 8.0

LANES = 128
VMEM_LIMIT = 48 * 1024 * 1024

W_G = N_BRANCH * D_MODEL
W_C = 2 * D_BRANCH
W_BM = 4 * D_BRANCH
W_BI = 384
W_A = 4 * D_BRANCH + 2 * LORA
OFF_G, OFF_C, OFF_BM, OFF_BI, OFF_A = 0, 3072, 4096, 6144, 6528
D_IN_PAD = OFF_A + W_A

NEG_BIG = -1e30
INT_MIN = -2 ** 31


def _dot(a, b):
    return jnp.dot(a.astype(BF16), b.astype(BF16), preferred_element_type=F32)


def _dot_nt(a, b):
    return lax.dot_general(a.astype(BF16), b.astype(BF16), (((1,), (1,)), ((), ())),
                           preferred_element_type=F32)


SMALL_DOT_PIECES = 1


def _pieces(x):
    hi = x.astype(BF16)
    if SMALL_DOT_PIECES == 1:
        return [hi]
    return [hi, (x - hi.astype(F32)).astype(BF16)]


def _sdot_general(a, b, dims):
    pa, pb = _pieces(a), _pieces(b)
    out = None
    for i, x in enumerate(pa):
        for j, y in enumerate(pb):
            if i + j < SMALL_DOT_PIECES:
                t = lax.dot_general(x, y, (dims, ((), ())), preferred_element_type=F32)
                out = t if out is None else out + t
    return out


def _sdot(a, b):
    return _sdot_general(a, b, ((1,), (0,)))


def _sdot_nt(a, b):
    return _sdot_general(a, b, ((1,), (1,)))


def _dot_sel(x, sel_bf16):
    hi = x.astype(BF16)
    r1 = x - hi.astype(F32)
    mid = r1.astype(BF16)
    lo = (r1 - mid.astype(F32)).astype(BF16)
    out = jnp.dot(hi, sel_bf16, preferred_element_type=F32)
    out = out + jnp.dot(mid, sel_bf16, preferred_element_type=F32)
    return out + jnp.dot(lo, sel_bf16, preferred_element_type=F32)


def _params(*sem):
    return pltpu.CompilerParams(dimension_semantics=sem, vmem_limit_bytes=VMEM_LIMIT)


def _inproj_body(h_ref, g_ref, w_ref, o_ref, hn_ref):
    @pl.when(pl.program_id(1) == 0)
    def _():
        x = h_ref[...]
        ms = jnp.mean(x * x, axis=-1, keepdims=True)
        hn_ref[...] = (x * lax.rsqrt(ms + NORM_EPS) * g_ref[...]).astype(BF16)

    o_ref[...] = jnp.dot(hn_ref[...], w_ref[...], preferred_element_type=F32)


def _inproj(h, g, w):
    T, D = h.shape
    N = w.shape[1]
    tm, tn = min(2048, T), 512
    return pl.pallas_call(
        _inproj_body,
        grid=(T // tm, N // tn),
        in_specs=[pl.BlockSpec((tm, D), lambda i, j: (i, 0)),
                  pl.BlockSpec((1, D), lambda i, j: (0, 0)),
                  pl.BlockSpec((D, tn), lambda i, j: (0, j))],
        out_specs=pl.BlockSpec((tm, tn), lambda i, j: (i, j)),
        out_shape=jax.ShapeDtypeStruct((T, N), F32),
        scratch_shapes=[pltpu.VMEM((tm, D), BF16)],
        compiler_params=_params("parallel", "arbitrary"),
        name="inproj",
    )(h, g, w)


def _rwkv_body(ua_ref, mu_ref, w0_ref, w2_ref, a0_ref, a2_ref, kk_ref, ka_ref, rk_ref,
               gng_ref, gnb_ref, hsel_ref, o_ref, z_ref, prev_ref, obuf_ref):
    NB, C, _ = ua_ref.shape
    R = NB * C
    DB = D_BRANCH

    @pl.when(pl.program_id(0) == 0)
    def _():
        z_ref[...] = jnp.zeros_like(z_ref)
        prev_ref[...] = jnp.zeros_like(prev_ref)

    ua = ua_ref[...].reshape(R, ua_ref.shape[2])
    row = lax.broadcasted_iota(I32, (R, 1), 0)
    shifted = pltpu.roll(ua, 1, 0)
    for bi in range(NB):
        shifted = jnp.where(row == bi * C, prev_ref[bi], shifted)
    for bi in range(NB):
        prev_ref[bi] = ua[(bi + 1) * C - 1:(bi + 1) * C, :]
    xs = ua + (shifted - ua) * mu_ref[...]

    r = xs[:, 0:DB]
    k = xs[:, DB:2 * DB]
    v = xs[:, 2 * DB:3 * DB]
    wl = xs[:, 3 * DB:3 * DB + LORA]
    al = xs[:, 3 * DB + LORA:3 * DB + 2 * LORA]
    g = xs[:, 3 * DB + 2 * LORA:]
    hsel = hsel_ref[...]

    w_pre = w0_ref[...] + _dot(jnp.tanh(wl), w2_ref[...])
    lw = -jnp.exp(-jax.nn.softplus(-w_pre) - 0.5)
    a = jax.nn.sigmoid(a0_ref[...] + _dot(al, a2_ref[...]))
    kk = k * kk_ref[...]
    k_mod = k * (1.0 + (a - 1.0) * ka_ref[...])
    sums = _dot_sel(jnp.concatenate([kk * kk, r * k_mod * rk_ref[...]], axis=0), hsel)
    kk = kk / jnp.maximum(jnp.sqrt(sums[0:R, :]), 1e-12)
    bonus = sums[R:2 * R, :] * v
    b = kk * a

    ri = lax.broadcasted_iota(I32, (R, R), 0)
    rj = lax.broadcasted_iota(I32, (R, R), 1)
    same_chunk = (ri >> CHUNK_SHIFT) == (rj >> CHUNK_SHIFT)
    cum_sel = jnp.where(same_chunk, jnp.where(ri >= rj, 1.0, 0.0), 0.0).astype(BF16)
    cl = _dot_sel_lhs(cum_sel, lw)
    cl_end = jnp.concatenate(
        [jnp.broadcast_to(cl[(bi + 1) * C - 1:(bi + 1) * C, :], (C, DB)) for bi in range(NB)], axis=0)
    g_incl = jnp.exp(cl)
    g_inv = jnp.exp(-cl)
    d_end = jnp.exp(cl_end - cl)
    kt_all = kk * jnp.exp(cl - lw)
    rt_all = r * g_incl
    bs_all = b * g_inv
    ks_all = k_mod * g_inv
    bh_all = b * d_end
    kh_all = k_mod * d_end

    ci = lax.broadcasted_iota(I32, (C, C), 0)
    cj = lax.broadcasted_iota(I32, (C, C), 1)
    incl = ci >= cj
    strict = ci > cj
    eye = ci == cj
    ident = jnp.where(eye, 1.0, 0.0)
    n_sq = int(np.log2(C)) - 1
    rows = [slice(bi * C, (bi + 1) * C) for bi in range(NB)]
    bh_t = [bh_all[rb, :].T for rb in rows]
    kh_t = [kh_all[rb, :].T for rb in rows]
    units = [(bi, h) for bi in range(NB) for h in range(N_HEADS)]
    un = range(len(units))
    hsl = [slice(h * HEAD_DIM, (h + 1) * HEAD_DIM) for _, h in units]
    rsl = [rows[bi] for bi, _ in units]
    kt = [kt_all[rsl[u], hsl[u]] for u in un]
    rt = [rt_all[rsl[u], hsl[u]] for u in un]
    vh = [v[rsl[u], hsl[u]] for u in un]
    gram = [_sdot_nt(jnp.concatenate([kt[u], rt[u]], axis=0),
                     jnp.concatenate([bs_all[rsl[u], hsl[u]], ks_all[rsl[u], hsl[u]]], axis=0))
            for u in un]
    b_m = [jnp.where(strict, gram[u][0:C, C:2 * C], 0.0) for u in un]
    ab = [jnp.where(incl, gram[u][C:2 * C, 0:C], 0.0) for u in un]
    bk = [jnp.where(incl, gram[u][C:2 * C, C:2 * C], 0.0) for u in un]
    pw = [jnp.where(strict, -gram[u][0:C, 0:C], 0.0) for u in un]
    tinv = [ident + pw[u] for u in un]
    bmv = [_sdot(b_m[u], vh[u]) for u in un]
    nv = [_sdot(jnp.concatenate([kh_t[units[u][0]][hsl[u], :], bk[u]], axis=0), vh[u]) for u in un]
    for _ in range(n_sq):
        pw = [_sdot(pw[u], pw[u]) for u in un]
        tinv = [_sdot(tinv[u], ident + pw[u]) for u in un]
    wu = [-_sdot(tinv[u], jnp.concatenate([kt[u], bmv[u]], axis=1)) for u in un]
    mq = [_sdot(jnp.concatenate([bh_t[units[u][0]][hsl[u], :], ab[u]], axis=0), wu[u]) for u in un]
    lhs = []
    for u in un:
        g_end = g_incl[rsl[u].stop - 1:rsl[u].stop, hsl[u]]
        m_m = jnp.where(eye, g_end, 0.0) + mq[u][0:HEAD_DIM, 0:HEAD_DIM]
        qe = rt[u] + mq[u][HEAD_DIM:, 0:HEAD_DIM]
        lhs.append(jnp.concatenate([qe, m_m], axis=0))
    oz = [_sdot(lhs[u], z_ref[u]) for u in un]
    for u in un:
        n_m = mq[u][0:HEAD_DIM, HEAD_DIM:] + nv[u][0:HEAD_DIM, :]
        o0 = mq[u][HEAD_DIM:, HEAD_DIM:] + nv[u][HEAD_DIM:, :]
        obuf_ref[rsl[u], hsl[u]] = oz[u][0:C, :] + o0
        z_ref[u] = oz[u][C:, :] + n_m

    o = obuf_ref[...]
    inv_n = 1.0 / HEAD_DIM
    mean = _dot_sel(o, hsel) * inv_n
    oc = o - mean
    var = _dot_sel(oc * oc, hsel) * inv_n
    o = oc * lax.rsqrt(var + RWKV_GN_EPS) * gng_ref[...] + gnb_ref[...]
    y = (o + bonus) * (g * jax.nn.sigmoid(g))
    o_ref[...] = y.reshape(NB, C, DB).astype(o_ref.dtype)


def _dot_sel_lhs(sel_bf16, x):
    hi = x.astype(BF16)
    r1 = x - hi.astype(F32)
    mid = r1.astype(BF16)
    lo = (r1 - mid.astype(F32)).astype(BF16)
    out = jnp.dot(sel_bf16, hi, preferred_element_type=F32)
    out = out + jnp.dot(sel_bf16, mid, preferred_element_type=F32)
    return out + jnp.dot(sel_bf16, lo, preferred_element_type=F32)


def _rwkv(u, B, S, mu, w0, w2, a0, a2, k_k, k_a, r_k, gn_g, gn_b, hsel):
    T = u.shape[0]
    C = CHUNK
    nc = S // C
    a_blk = OFF_A // W_A
    vec = lambda n: pl.BlockSpec((1, n), lambda c: (0, 0))
    mat = lambda m, n: pl.BlockSpec((m, n), lambda c: (0, 0))
    out = pl.pallas_call(
        _rwkv_body,
        grid=(nc,),
        in_specs=[pl.BlockSpec((B, C, W_A), lambda c: (0, c, a_blk)),
                  vec(W_A), vec(D_BRANCH), mat(LORA, D_BRANCH), vec(D_BRANCH), mat(LORA, D_BRANCH),
                  vec(D_BRANCH), vec(D_BRANCH), vec(D_BRANCH), vec(D_BRANCH), vec(D_BRANCH),
                  mat(D_BRANCH, D_BRANCH)],
        out_specs=pl.BlockSpec((B, C, D_BRANCH), lambda c: (0, c, 0)),
        out_shape=jax.ShapeDtypeStruct((B, S, D_BRANCH), BF16),
        scratch_shapes=[pltpu.VMEM((B * N_HEADS, HEAD_DIM, HEAD_DIM), F32),
                        pltpu.VMEM((B, 1, W_A), F32),
                        pltpu.VMEM((B * C, D_BRANCH), F32)],
        compiler_params=_params("arbitrary"),
        name="rwkv",
    )(u.reshape(B, S, u.shape[1]), mu, w0, w2, a0, a2, k_k, k_a, r_k, gn_g, gn_b, hsel)
    return out.reshape(T, D_BRANCH)


def _rope(x, cosw, sinw):
    W = x.shape[1]
    lane = lax.broadcasted_iota(I32, (1, W), 1) & (HEAD_DIM - 1)
    half = ROPE_DIM // 2
    partner = jnp.where(lane < half, pltpu.roll(x, W - half, 1), pltpu.roll(x, half, 1))
    return x * cosw + partner * sinw


def _dsa_prep_body(um_ref, ui_ref, cos_ref, sin_ref, qg_ref, kg_ref, hsel_ref,
                   qn_ref, kn_ref, vt_ref, qi_ref, ki_ref, wt_ref):
    DB = D_BRANCH
    cos1, sin1 = cos_ref[...], sin_ref[...]
    cos2, sin2 = jnp.concatenate([cos1] * 2, axis=1), jnp.concatenate([sin1] * 2, axis=1)
    cos4, sin4 = jnp.concatenate([cos2] * 2, axis=1), jnp.concatenate([sin2] * 2, axis=1)
    hsel = hsel_ref[...]

    def headnorm(x, gain):
        ms = _dot_sel(x * x, hsel) * (1.0 / HEAD_DIM)
        return x * lax.rsqrt(ms + NORM_EPS) * gain

    q = _rope(headnorm(um_ref[:, 0:DB], qg_ref[...]), cos4, sin4)
    k = _rope(headnorm(um_ref[:, DB:2 * DB], kg_ref[...]), cos4, sin4)
    qn_ref[...] = (q * (HEAD_DIM ** -0.5 * LOG2E)).astype(BF16)
    kn_ref[...] = k.astype(BF16)
    vt_ref[...] = um_ref[:, 2 * DB:3 * DB].T.astype(BF16)
    qi_ref[...] = _rope(ui_ref[:, 0:IDX_HEADS * IDX_DIM], cos2, sin2).astype(BF16)
    kw = ui_ref[:, IDX_HEADS * IDX_DIM:]
    ki_ref[...] = _rope(kw, cos1, sin1)[:, 0:IDX_DIM].astype(BF16)
    wt_ref[...] = (kw * (IDX_HEADS ** -0.5 * IDX_DIM ** -0.5)).T


def _dsa_prep(u, B, S, cosf, sinf, qg, kg, hsel):
    T = u.shape[0]
    tm = min(512, S)
    ns = S // tm
    vec = lambda n: pl.BlockSpec((1, n), lambda i: (0, 0))
    row = lambda n: pl.BlockSpec((tm, n), lambda i: (i, 0))
    col = lambda n: pl.BlockSpec((None, n, tm), lambda i: (i // ns, 0, i % ns))
    return pl.pallas_call(
        _dsa_prep_body,
        grid=(T // tm,),
        in_specs=[pl.BlockSpec((tm, W_BM), lambda i: (i, OFF_BM // W_BM)),
                  pl.BlockSpec((tm, W_BI), lambda i: (i, OFF_BI // W_BI)),
                  row(LANES), row(LANES), vec(D_BRANCH), vec(D_BRANCH),
                  pl.BlockSpec((D_BRANCH, D_BRANCH), lambda i: (0, 0))],
        out_specs=[row(D_BRANCH), row(D_BRANCH), col(D_BRANCH), row(IDX_HEADS * IDX_DIM),
                   row(IDX_DIM), col(LANES)],
        out_shape=[jax.ShapeDtypeStruct((T, D_BRANCH), BF16),
                   jax.ShapeDtypeStruct((T, D_BRANCH), BF16),
                   jax.ShapeDtypeStruct((B, D_BRANCH, S), BF16),
                   jax.ShapeDtypeStruct((T, IDX_HEADS * IDX_DIM), BF16),
                   jax.ShapeDtypeStruct((T, IDX_DIM), BF16),
                   jax.ShapeDtypeStruct((B, LANES, S), F32)],
        compiler_params=_params("parallel"),
        name="dsa_prep",
    )(u, u, cosf, sinf, qg, kg, hsel)


def _dsa_body(qn_ref, qi_ref, wt_ref, g_ref, kn_ref, vt_ref, ki_ref, o_ref,
              key_ref, khi_ref, klo_ref, m_ref, l_ref, acc_ref, *, ksel, kt):
    QB = qn_ref.shape[0]
    S = kn_ref.shape[0]
    SUB = 8
    q0 = pl.program_id(1) * QB
    qlane = lax.broadcasted_iota(I32, (1, QB), 1)
    lim = (((q0 + qlane) >> CHUNK_SHIFT) + 1) << CHUNK_SHIFT
    n_t = (q0 + QB + kt - 1) // kt
    krow = lax.broadcasted_iota(I32, (kt, 1), 0)
    imin = jnp.int32(INT_MIN)
    qi = qi_ref[...]
    qi_h = [qi[:, h * IDX_DIM:(h + 1) * IDX_DIM] for h in range(IDX_HEADS)]
    w_h = [wt_ref[IDX_DIM + h:IDX_DIM + h + 1, :] for h in range(IDX_HEADS)]

    def fold(x, op):
        return op(x.reshape(kt // SUB, SUB, QB), axis=0)

    def score_tile(t, carry):
        off = pl.multiple_of(t * kt, kt)
        ki_t = ki_ref[pl.ds(off, kt), :]
        sc = jnp.zeros((kt, QB), F32)
        for h in range(IDX_HEADS):
            d = lax.dot_general(ki_t, qi_h[h], (((1,), (1,)), ((), ())),
                                preferred_element_type=F32)
            sc = sc + w_h[h] * jnp.maximum(d, 0.0)
        bits = pltpu.bitcast(sc, I32)
        key = bits ^ ((bits >> 31) & jnp.int32(0x7FFFFFFF))
        key = jnp.where(off + krow < lim, key, imin)
        key_ref[pl.ds(off, kt), :] = key
        khi_ref[pl.ds(off, kt), :] = (key >> 16).astype(I16)
        klo_ref[pl.ds(off, kt), :] = ((key & 0xFFFF) - HALF).astype(I16)
        return carry

    lax.fori_loop(0, n_t, score_tile, 0)

    def count16(ref, pred):
        n_acc = 4

        def body(t, accs):
            off = pl.multiple_of(t * kt, kt)
            accs = list(accs)
            x = ref[pl.ds(off, kt), :]
            for j in range(kt // PACK):
                hit = jnp.where(pred(x[j * PACK:(j + 1) * PACK, :]), jnp.int16(1), jnp.int16(0))
                accs[j % n_acc] = accs[j % n_acc] + hit
            return tuple(accs)

        zero = jnp.zeros((PACK, QB), I16)
        accs = lax.fori_loop(0, n_t, body, (zero,) * n_acc)
        acc = (accs[0] + accs[1]) + (accs[2] + accs[3])
        return jnp.sum(acc.astype(I32), axis=0, keepdims=True)

    def bisect16(ref, target):
        def body(i, ulo):
            ucand = ulo + jnp.left_shift(jnp.int32(1), 15 - i)
            cand = (ucand - HALF).astype(I16)
            cnt = count16(ref, lambda x: x >= cand)
            return jnp.where(cnt >= target, ucand, ulo)

        return lax.fori_loop(0, 16, body, jnp.zeros((1, QB), I32))

    def count(pred):
        def body(t, acc):
            off = pl.multiple_of(t * kt, kt)
            hit = jnp.where(pred(key_ref[pl.ds(off, kt), :], off), 1, 0)
            return acc + fold(hit, jnp.sum)

        acc = lax.fori_loop(0, n_t, body, jnp.zeros((SUB, QB), I32))
        return jnp.sum(acc, axis=0, keepdims=True)

    uhi = bisect16(khi_ref, ksel)
    tau_hi = (uhi - HALF).astype(I16)
    k_rest = ksel - count16(khi_ref, lambda x: x > tau_hi)

    def mask_low(t, carry):
        off = pl.multiple_of(t * kt, kt)
        sl = pl.ds(off, kt)
        klo_ref[sl, :] = jnp.where(khi_ref[sl, :] == tau_hi, klo_ref[sl, :], jnp.int16(-HALF))
        return carry

    lax.fori_loop(0, n_t, mask_low, 0)
    ulo = bisect16(klo_ref, k_rest)
    tau = ((uhi - HALF) << 16) | ulo

    need = ksel - count(lambda key, off: key > tau)
    n_bits = int(np.log2(S))

    def tie_bisect(i, jlo):
        cand = jlo + jnp.left_shift(jnp.int32(1), n_bits - 1 - i)
        cnt = count(lambda key, off: jnp.where(key == tau, off + krow, S) < cand)
        return jnp.where(cnt < need, cand, jlo)

    has_ties = jnp.max(count(lambda key, off: key >= tau)) > ksel
    jmax = lax.cond(has_ties,
                    lambda: lax.fori_loop(0, n_bits, tie_bisect, jnp.zeros((1, QB), I32)),
                    lambda: jnp.full((1, QB), S, I32))

    m_ref[...] = jnp.full_like(m_ref, NEG_BIG)
    l_ref[...] = jnp.zeros_like(l_ref)
    acc_ref[...] = jnp.zeros_like(acc_ref)
    low_half = lax.broadcasted_iota(I32, (1, LANES), 1) < HEAD_DIM
    qn = qn_ref[...].astype(F32)
    q_pair = []
    for h in range(N_HEADS):
        pair = qn[:, (h // 2) * LANES:(h // 2 + 1) * LANES]
        own = low_half if h % 2 == 0 else jnp.logical_not(low_half)
        q_pair.append(jnp.where(own, pair, 0.0).astype(BF16))
    ONES_ROWS = 16
    ones = jnp.ones((ONES_ROWS, kt), BF16)
    tau_c = jnp.maximum(tau, imin + 1)

    def att_tile(t, carry):
        off = pl.multiple_of(t * kt, kt)
        key = key_ref[pl.ds(off, kt), :]
        tie_ok = jnp.where(key == tau_c, off + krow, S) <= jmax
        bias = jnp.where(key > tau_c, 0.0, jnp.where(tie_ok, 0.0, NEG_BIG))
        for g0 in range(0, N_HEADS, HEAD_GROUP):
            heads = range(g0, g0 + HEAD_GROUP)
            pairs = {h // 2: slice((h // 2) * LANES, (h // 2 + 1) * LANES) for h in heads}
            k_p = {pr: kn_ref[pl.ds(off, kt), ps] for pr, ps in pairs.items()}
            vo_p = {pr: jnp.concatenate([vt_ref[ps, pl.ds(off, kt)], ones], axis=0)
                    for pr, ps in pairs.items()}
            s = {h: lax.dot_general(k_p[h // 2], q_pair[h], (((1,), (1,)), ((), ())),
                                    preferred_element_type=F32) + bias for h in heads}
            m_old = {h: m_ref[h] for h in heads}
            m_new = {h: jnp.maximum(m_old[h], jnp.max(fold(s[h], jnp.max), axis=0, keepdims=True))
                     for h in heads}
            p = {h: jnp.exp2(s[h] - m_new[h][0:1, :]).astype(BF16) for h in heads}
            alpha = {h: jnp.exp2(m_old[h] - m_new[h]) for h in heads}
            pvl = {h: jnp.dot(vo_p[h // 2], p[h], preferred_element_type=F32) for h in heads}
            for h in heads:
                l_ref[h] = alpha[h] * l_ref[h] + pvl[h][LANES:LANES + SUB, :]
                m_ref[h] = m_new[h]
                hs = slice(h * HEAD_DIM, (h + 1) * HEAD_DIM)
                own = slice((h % 2) * HEAD_DIM, (h % 2 + 1) * HEAD_DIM)
                acc_ref[hs, :] = alpha[h][0:1, :] * acc_ref[hs, :] + pvl[h][own, :]
        return carry

    lax.fori_loop(0, n_t, att_tile, 0)

    o_t = jnp.concatenate(
        [acc_ref[h * HEAD_DIM:(h + 1) * HEAD_DIM, :] / l_ref[h][0:1, :] for h in range(N_HEADS)],
        axis=0)
    g = g_ref[...]
    o_ref[...] = (o_t.T * (g * jax.nn.sigmoid(g))).astype(o_ref.dtype)


def _dsa(u, qn, kn, vt, qi, ki, wt, B, S, ksel):
    T = u.shape[0]
    QB = min(256, S)
    kt = min(512, S)
    nq = S // QB
    g_blk = (OFF_BM + 3 * D_BRANCH) // D_BRANCH
    qrow = lambda n: pl.BlockSpec((QB, n), lambda b, i: (b * nq + i, 0))
    full = lambda n: pl.BlockSpec((S, n), lambda b, i: (b, 0))
    return pl.pallas_call(
        functools.partial(_dsa_body, ksel=ksel, kt=kt),
        grid=(B, nq),
        in_specs=[qrow(D_BRANCH), qrow(IDX_HEADS * IDX_DIM),
                  pl.BlockSpec((None, LANES, QB), lambda b, i: (b, 0, i)),
                  pl.BlockSpec((QB, D_BRANCH), lambda b, i: (b * nq + i, g_blk)),
                  full(D_BRANCH),
                  pl.BlockSpec((None, D_BRANCH, S), lambda b, i: (b, 0, 0)),
                  full(IDX_DIM)],
        out_specs=qrow(D_BRANCH),
        out_shape=jax.ShapeDtypeStruct((T, D_BRANCH), BF16),
        scratch_shapes=[pltpu.VMEM((S, QB), I32),
                        pltpu.VMEM((S, QB), I16),
                        pltpu.VMEM((S, QB), I16),
                        pltpu.VMEM((N_HEADS, 8, QB), F32),
                        pltpu.VMEM((N_HEADS, 8, QB), F32),
                        pltpu.VMEM((D_BRANCH, QB), F32)],
        compiler_params=_params("parallel", "arbitrary"),
        name="dsa",
    )(qn, qi, wt, u, kn, vt, ki)


def _lru_body(u_ref, cw_ref, cb_ref, wri_ref, bri_ref, lam_ref, o_ref, xbuf_ref, h_ref):
    TS = u_ref.shape[0]
    DB = D_BRANCH
    PAD = 8

    @pl.when(pl.program_id(1) == 0)
    def _():
        xbuf_ref[0:PAD, :] = jnp.zeros((PAD, DB), F32)
        h_ref[...] = jnp.zeros_like(h_ref)

    x = u_ref[:, 0:DB]
    g = u_ref[:, DB:2 * DB]
    xbuf_ref[PAD:PAD + TS, :] = x
    xc = cb_ref[...]
    for j in range(CONV_WIDTH):
        xc = xc + cw_ref[j:j + 1, :] * xbuf_ref[pl.ds(PAD - (CONV_WIDTH - 1) + j, TS), :]
    xbuf_ref[0:PAD, :] = x[TS - PAD:TS, :]

    ri = jax.nn.sigmoid(_dot(xc, wri_ref[...]) + bri_ref[...])
    r, i = ri[:, 0:DB], ri[:, DB:2 * DB]
    log_a = (-LRU_C) * r * jax.nn.softplus(-lam_ref[...])
    a = jnp.exp(log_a)
    b = jnp.sqrt(jnp.tanh(-log_a) * (1.0 + a * a)) * (i * xc)

    row = lax.broadcasted_iota(I32, (TS, 1), 0)
    step = 1
    while step < TS:
        live = row >= step
        a_s = jnp.where(live, pltpu.roll(a, step, 0), 1.0)
        b_s = jnp.where(live, pltpu.roll(b, step, 0), 0.0)
        b = a * b_s + b
        a = a * a_s
        step *= 2
    h = b + a * h_ref[...]
    h_ref[...] = h[TS - 1:TS, :]
    o_ref[...] = (h * (g * jax.nn.sigmoid(g))).astype(o_ref.dtype)


def _lru(u, B, S, conv_w, conv_b, w_ri, b_ri, lam):
    T = u.shape[0]
    TS = min(256, S)
    ns = S // TS
    vec = lambda n: pl.BlockSpec((1, n), lambda b, s: (0, 0))
    return pl.pallas_call(
        _lru_body,
        grid=(B, ns),
        in_specs=[pl.BlockSpec((TS, W_C), lambda b, s: (b * ns + s, OFF_C // W_C)),
                  pl.BlockSpec((CONV_WIDTH, D_BRANCH), lambda b, s: (0, 0)),
                  vec(D_BRANCH),
                  pl.BlockSpec((D_BRANCH, 2 * D_BRANCH), lambda b, s: (0, 0)),
                  vec(2 * D_BRANCH), vec(D_BRANCH)],
        out_specs=pl.BlockSpec((TS, D_BRANCH), lambda b, s: (b * ns + s, 0)),
        out_shape=jax.ShapeDtypeStruct((T, D_BRANCH), BF16),
        scratch_shapes=[pltpu.VMEM((TS + 8, D_BRANCH), F32), pltpu.VMEM((1, D_BRANCH), F32)],
        compiler_params=_params("parallel", "arbitrary"),
        name="lru",
    )(u, conv_w, conv_b, w_ri, b_ri, lam)


def _merge_body(ya_ref, yb_ref, yc_ref, ug_ref, h_ref, p_ref, wb_ref, wo_ref, wp_ref, wpg_ref,
                o_ref):
    D = D_MODEL
    merged = None
    for n, y_ref in enumerate((ya_ref, yb_ref, yc_ref)):
        term = jax.nn.sigmoid(ug_ref[:, n * D:(n + 1) * D]) * _dot(y_ref[...], wb_ref[n])
        merged = term if merged is None else merged + term
    h1 = h_ref[...] + _dot(merged, wo_ref[...])
    gate = jax.nn.sigmoid(_dot(h1, wpg_ref[...]))
    o_ref[...] = h1 + gate * _dot(p_ref[...], wp_ref[...])


def _merge(ya, yb, yc, u, h, p, wb, wo, wp, wpg):
    T, D = h.shape
    tm = min(256, T)
    row = lambda n: pl.BlockSpec((tm, n), lambda i: (i, 0))
    return pl.pallas_call(
        _merge_body,
        grid=(T // tm,),
        in_specs=[row(D_BRANCH), row(D_BRANCH), row(D_BRANCH),
                  pl.BlockSpec((tm, W_G), lambda i: (i, OFF_G // W_G)),
                  row(D), row(PLE_DIM),
                  pl.BlockSpec((N_BRANCH, D_BRANCH, D), lambda i: (0, 0, 0)),
                  pl.BlockSpec((D, D), lambda i: (0, 0)),
                  pl.BlockSpec((PLE_DIM, D), lambda i: (0, 0)),
                  pl.BlockSpec((D, D), lambda i: (0, 0))],
        out_specs=row(D),
        out_shape=jax.ShapeDtypeStruct((T, D), F32),
        compiler_params=_params("parallel"),
        name="merge",
    )(ya, yb, yc, u, h, p, wb, wo, wp, wpg)


def _reorder_w_in(w):
    a0 = 0
    b0 = a0 + W_A
    c0 = b0 + 3 * D_BRANCH + IDX_HEADS * IDX_DIM + IDX_DIM + IDX_HEADS + D_BRANCH
    g0 = c0 + W_C
    bq = w[:, b0:b0 + 3 * D_BRANCH]
    bi0 = b0 + 3 * D_BRANCH
    n_idx = IDX_HEADS * IDX_DIM + IDX_DIM + IDX_HEADS
    bidx = w[:, bi0:bi0 + n_idx]
    bg = w[:, bi0 + n_idx:c0]
    pad = jnp.zeros((w.shape[0], W_BI - n_idx), w.dtype)
    out = jnp.concatenate([w[:, g0:g0 + W_G], w[:, c0:g0], bq, bg, bidx, pad, w[:, a0:b0]], axis=1)
    return _to_bf16(out)


def _cast_body(x_ref, o_ref):
    o_ref[...] = x_ref[...].astype(o_ref.dtype)


def _to_bf16(w):
    R, N = w.shape
    tn = 512
    return pl.pallas_call(
        _cast_body,
        grid=(N // tn,),
        in_specs=[pl.BlockSpec((R, tn), lambda j: (0, j))],
        out_specs=pl.BlockSpec((R, tn), lambda j: (0, j)),
        out_shape=jax.ShapeDtypeStruct((R, N), BF16),
        compiler_params=_params("parallel"),
        name="cast_bf16",
    )(w)


def _block_diag(w):
    H, n, _ = w.shape
    eye = jnp.eye(H, dtype=w.dtype)
    return (eye[:, None, :, None] * w[:, :, None, :]).reshape(H * n, H * n)


def _rope_tables(positions):
    half = ROPE_DIM // 2
    inv_freq = ROPE_THETA ** (-jnp.arange(0, ROPE_DIM, 2, dtype=F32) / ROPE_DIM)
    ang = positions.astype(F32).reshape(-1, 1) * inv_freq
    cos, sin = jnp.cos(ang), jnp.sin(ang)
    T = cos.shape[0]
    rest = HEAD_DIM - ROPE_DIM
    cos_h = jnp.concatenate([cos, cos, jnp.ones((T, rest), F32)], axis=1)
    sin_h = jnp.concatenate([-sin, sin, jnp.zeros((T, rest), F32)], axis=1)
    return jnp.tile(cos_h, (1, LANES // HEAD_DIM)), jnp.tile(sin_h, (1, LANES // HEAD_DIM))


def kernel(x, p, positions, norm_g, w_in, rwkv_mu, rwkv_w0, rwkv_w2, rwkv_a0, rwkv_a2, rwkv_k_k,
           rwkv_k_a, rwkv_r_k, rwkv_gn_g, rwkv_gn_b, dsa_q_g, dsa_k_g, lru_conv_w, lru_conv_b,
           lru_w_r, lru_b_r, lru_w_i, lru_b_i, lru_lambda, w_branch, w_out, w_ple, w_ple_gate):
    B, S, D = x.shape
    depth = w_in.shape[0]
    T = B * S
    ksel = min(TOPK_MAX, S // 4)
    cosf, sinf = _rope_tables(positions)
    head_id = np.arange(D_BRANCH) // HEAD_DIM
    hsel = jnp.asarray(head_id[:, None] == head_id[None, :], BF16)
    row = lambda v: v.reshape(1, -1)
    per_head = lambda v: jnp.tile(v, N_HEADS).reshape(1, -1)

    h = x.reshape(T, D)
    for i in range(depth):
        u = _inproj(h, row(norm_g[i]), _reorder_w_in(w_in[i]))
        ya = _rwkv(u, B, S, row(rwkv_mu[i]), row(rwkv_w0[i]), rwkv_w2[i], row(rwkv_a0[i]),
                   rwkv_a2[i], row(rwkv_k_k[i]), row(rwkv_k_a[i]), row(rwkv_r_k[i]),
                   row(rwkv_gn_g[i]), row(rwkv_gn_b[i]), hsel)
        qn, kn, vt, qi, ki, wt = _dsa_prep(u, B, S, cosf, sinf, per_head(dsa_q_g[i]),
                                           per_head(dsa_k_g[i]), hsel)
        yb = _dsa(u, qn, kn, vt, qi, ki, wt, B, S, ksel)
        w_ri = jnp.concatenate([_block_diag(lru_w_r[i]), _block_diag(lru_w_i[i])], axis=1)
        b_ri = jnp.concatenate([lru_b_r[i], lru_b_i[i]]).reshape(1, -1)
        yc = _lru(u, B, S, lru_conv_w[i], row(lru_conv_b[i]), w_ri.astype(BF16), b_ri,
                  row(lru_lambda[i]))
        h = _merge(ya, yb, yc, u, h, p[i].reshape(T, PLE_DIM), w_branch[i].astype(BF16),
                   w_out[i].astype(BF16), w_ple[i].astype(BF16), w_ple_gate[i].astype(BF16))
    return h.reshape(B, S, D)
```

```python
import functools

import numpy as np
import jax
import jax.numpy as jnp
from jax import lax
from jax.experimental import pallas as pl
from jax.experimental.pallas import tpu as pltpu

F32 = jnp.float32
BF16 = jnp.bfloat16
I32 = jnp.int32
I16 = jnp.int16
HALF = 1 << 15
PACK = 16
LOG2E = 1.4426950408889634
HEAD_GROUP = 8

D_MODEL = 1024
D_BRANCH = 512
HEAD_DIM = 64
N_HEADS = 8
N_BRANCH = 3
PLE_DIM = 256
CHUNK = 64
CHUNK_SHIFT = 6
NORM_EPS = 1e-6
LORA = 64
RWKV_GN_EPS = 64e-5
IDX_HEADS = 4
IDX_DIM = 64
TOPK_MAX = 256
ROPE_THETA = 500000.0
ROPE_DIM = 16
CONV_WIDTH = 4
LRU_C = 8.0

LANES = 128
VMEM_LIMIT = 48 * 1024 * 1024

W_G = N_BRANCH * D_MODEL
W_C = 2 * D_BRANCH
W_BM = 4 * D_BRANCH
W_BI = 384
W_A = 4 * D_BRANCH + 2 * LORA
OFF_G, OFF_C, OFF_BM, OFF_BI, OFF_A = 0, 3072, 4096, 6144, 6528
D_IN_PAD = OFF_A + W_A

NEG_BIG = -1e30
INT_MIN = -2 ** 31


def _dot(a, b):
    return jnp.dot(a.astype(BF16), b.astype(BF16), preferred_element_type=F32)


def _dot_nt(a, b):
    return lax.dot_general(a.astype(BF16), b.astype(BF16), (((1,), (1,)), ((), ())),
                           preferred_element_type=F32)


SMALL_DOT_PIECES = 1


def _pieces(x):
    hi = x.astype(BF16)
    if SMALL_DOT_PIECES == 1:
        return [hi]
    return [hi, (x - hi.astype(F32)).astype(BF16)]


def _sdot_general(a, b, dims):
    pa, pb = _pieces(a), _pieces(b)
    out = None
    for i, x in enumerate(pa):
        for j, y in enumerate(pb):
            if i + j < SMALL_DOT_PIECES:
                t = lax.dot_general(x, y, (dims, ((), ())), preferred_element_type=F32)
                out = t if out is None else out + t
    return out


def _sdot(a, b):
    return _sdot_general(a, b, ((1,), (0,)))


def _sdot_nt(a, b):
    return _sdot_general(a, b, ((1,), (1,)))


def _dot_sel(x, sel_bf16):
    hi = x.astype(BF16)
    r1 = x - hi.astype(F32)
    mid = r1.astype(BF16)
    lo = (r1 - mid.astype(F32)).astype(BF16)
    out = jnp.dot(hi, sel_bf16, preferred_element_type=F32)
    out = out + jnp.dot(mid, sel_bf16, preferred_element_type=F32)
    return out + jnp.dot(lo, sel_bf16, preferred_element_type=F32)


def _params(*sem):
    return pltpu.CompilerParams(dimension_semantics=sem, vmem_limit_bytes=VMEM_LIMIT)


def _inproj_body(h_ref, g_ref, w_ref, o_ref, hn_ref):
    @pl.when(pl.program_id(1) == 0)
    def _():
        x = h_ref[...]
        ms = jnp.mean(x * x, axis=-1, keepdims=True)
        hn_ref[...] = (x * lax.rsqrt(ms + NORM_EPS) * g_ref[...]).astype(BF16)

    o_ref[...] = jnp.dot(hn_ref[...], w_ref[...], preferred_element_type=F32).astype(o_ref.dtype)


def _inproj(h, g, w):
    T, D = h.shape
    N = w.shape[1]
    tm, tn = min(2048, T), 512
    return pl.pallas_call(
        _inproj_body,
        grid=(T // tm, N // tn),
        in_specs=[pl.BlockSpec((tm, D), lambda i, j: (i, 0)),
                  pl.BlockSpec((1, D), lambda i, j: (0, 0)),
                  pl.BlockSpec((D, tn), lambda i, j: (0, j))],
        out_specs=pl.BlockSpec((tm, tn), lambda i, j: (i, j)),
        out_shape=jax.ShapeDtypeStruct((T, N), BF16),
        scratch_shapes=[pltpu.VMEM((tm, D), BF16)],
        compiler_params=_params("parallel", "arbitrary"),
        name="inproj",
    )(h, g, w)


def _rwkv_body(ua_ref, mu_ref, w0_ref, w2_ref, a0_ref, a2_ref, kk_ref, ka_ref, rk_ref,
               gng_ref, gnb_ref, hsel_ref, o_ref, z_ref, prev_ref, obuf_ref):
    NB, C, _ = ua_ref.shape
    R = NB * C
    DB = D_BRANCH

    @pl.when(pl.program_id(0) == 0)
    def _():
        z_ref[...] = jnp.zeros_like(z_ref)
        prev_ref[...] = jnp.zeros_like(prev_ref)

    ua = ua_ref[...].astype(F32).reshape(R, ua_ref.shape[2])
    row = lax.broadcasted_iota(I32, (R, 1), 0)
    shifted = pltpu.roll(ua, 1, 0)
    for bi in range(NB):
        shifted = jnp.where(row == bi * C, prev_ref[bi], shifted)
    for bi in range(NB):
        prev_ref[bi] = ua[(bi + 1) * C - 1:(bi + 1) * C, :]
    xs = ua + (shifted - ua) * mu_ref[...]

    r = xs[:, 0:DB]
    k = xs[:, DB:2 * DB]
    v = xs[:, 2 * DB:3 * DB]
    wl = xs[:, 3 * DB:3 * DB + LORA]
    al = xs[:, 3 * DB + LORA:3 * DB + 2 * LORA]
    g = xs[:, 3 * DB + 2 * LORA:]
    hsel = hsel_ref[...]

    w_pre = w0_ref[...] + _dot(jnp.tanh(wl), w2_ref[...])
    lw = -jnp.exp(-jax.nn.softplus(-w_pre) - 0.5)
    a = jax.nn.sigmoid(a0_ref[...] + _dot(al, a2_ref[...]))
    kk = k * kk_ref[...]
    k_mod = k * (1.0 + (a - 1.0) * ka_ref[...])
    sums = _dot_sel(jnp.concatenate([kk * kk, r * k_mod * rk_ref[...]], axis=0), hsel)
    kk = kk / jnp.maximum(jnp.sqrt(sums[0:R, :]), 1e-12)
    bonus = sums[R:2 * R, :] * v
    b = kk * a

    ri = lax.broadcasted_iota(I32, (R, R), 0)
    rj = lax.broadcasted_iota(I32, (R, R), 1)
    same_chunk = (ri >> CHUNK_SHIFT) == (rj >> CHUNK_SHIFT)
    cum_sel = jnp.where(same_chunk, jnp.where(ri >= rj, 1.0, 0.0), 0.0).astype(BF16)
    cl = _dot_sel_lhs(cum_sel, lw)
    cl_end = jnp.concatenate(
        [jnp.broadcast_to(cl[(bi + 1) * C - 1:(bi + 1) * C, :], (C, DB)) for bi in range(NB)], axis=0)
    g_incl = jnp.exp(cl)
    g_inv = jnp.exp(-cl)
    d_end = jnp.exp(cl_end - cl)
    kt_all = kk * jnp.exp(cl - lw)
    rt_all = r * g_incl
    bs_all = b * g_inv
    ks_all = k_mod * g_inv
    bh_all = b * d_end
    kh_all = k_mod * d_end

    ci = lax.broadcasted_iota(I32, (C, C), 0)
    cj = lax.broadcasted_iota(I32, (C, C), 1)
    incl = ci >= cj
    strict = ci > cj
    eye = ci == cj
    ident = jnp.where(eye, 1.0, 0.0)
    n_sq = int(np.log2(C)) - 1
    rows = [slice(bi * C, (bi + 1) * C) for bi in range(NB)]
    bh_t = [bh_all[rb, :].T for rb in rows]
    kh_t = [kh_all[rb, :].T for rb in rows]
    units = [(bi, h) for bi in range(NB) for h in range(N_HEADS)]
    un = range(len(units))
    hsl = [slice(h * HEAD_DIM, (h + 1) * HEAD_DIM) for _, h in units]
    rsl = [rows[bi] for bi, _ in units]
    kt = [kt_all[rsl[u], hsl[u]] for u in un]
    rt = [rt_all[rsl[u], hsl[u]] for u in un]
    vh = [v[rsl[u], hsl[u]] for u in un]
    gram = [_sdot_nt(jnp.concatenate([kt[u], rt[u]], axis=0),
                     jnp.concatenate([bs_all[rsl[u], hsl[u]], ks_all[rsl[u], hsl[u]]], axis=0))
            for u in un]
    b_m = [jnp.where(strict, gram[u][0:C, C:2 * C], 0.0) for u in un]
    ab = [jnp.where(incl, gram[u][C:2 * C, 0:C], 0.0) for u in un]
    bk = [jnp.where(incl, gram[u][C:2 * C, C:2 * C], 0.0) for u in un]
    pw = [jnp.where(strict, -gram[u][0:C, 0:C], 0.0) for u in un]
    tinv = [ident + pw[u] for u in un]
    bmv = [_sdot(b_m[u], vh[u]) for u in un]
    nv = [_sdot(jnp.concatenate([kh_t[units[u][0]][hsl[u], :], bk[u]], axis=0), vh[u]) for u in un]
    for _ in range(n_sq):
        pw = [_sdot(pw[u], pw[u]) for u in un]
        tinv = [_sdot(tinv[u], ident + pw[u]) for u in un]
    wu = [-_sdot(tinv[u], jnp.concatenate([kt[u], bmv[u]], axis=1)) for u in un]
    mq = [_sdot(jnp.concatenate([bh_t[units[u][0]][hsl[u], :], ab[u]], axis=0), wu[u]) for u in un]
    lhs = []
    for u in un:
        g_end = g_incl[rsl[u].stop - 1:rsl[u].stop, hsl[u]]
        m_m = jnp.where(eye, g_end, 0.0) + mq[u][0:HEAD_DIM, 0:HEAD_DIM]
        qe = rt[u] + mq[u][HEAD_DIM:, 0:HEAD_DIM]
        lhs.append(jnp.concatenate([qe, m_m], axis=0))
    oz = [_sdot(lhs[u], z_ref[u]) for u in un]
    for u in un:
        n_m = mq[u][0:HEAD_DIM, HEAD_DIM:] + nv[u][0:HEAD_DIM, :]
        o0 = mq[u][HEAD_DIM:, HEAD_DIM:] + nv[u][HEAD_DIM:, :]
        obuf_ref[rsl[u], hsl[u]] = oz[u][0:C, :] + o0
        z_ref[u] = oz[u][C:, :] + n_m

    o = obuf_ref[...]
    inv_n = 1.0 / HEAD_DIM
    mean = _dot_sel(o, hsel) * inv_n
    oc = o - mean
    var = _dot_sel(oc * oc, hsel) * inv_n
    o = oc * lax.rsqrt(var + RWKV_GN_EPS) * gng_ref[...] + gnb_ref[...]
    y = (o + bonus) * (g * jax.nn.sigmoid(g))
    o_ref[...] = y.reshape(NB, C, DB).astype(o_ref.dtype)


def _dot_sel_lhs(sel_bf16, x):
    hi = x.astype(BF16)
    r1 = x - hi.astype(F32)
    mid = r1.astype(BF16)
    lo = (r1 - mid.astype(F32)).astype(BF16)
    out = jnp.dot(sel_bf16, hi, preferred_element_type=F32)
    out = out + jnp.dot(sel_bf16, mid, preferred_element_type=F32)
    return out + jnp.dot(sel_bf16, lo, preferred_element_type=F32)


def _rwkv(u, B, S, mu, w0, w2, a0, a2, k_k, k_a, r_k, gn_g, gn_b, hsel):
    T = u.shape[0]
    C = CHUNK
    nc = S // C
    a_blk = OFF_A // W_A
    vec = lambda n: pl.BlockSpec((1, n), lambda c: (0, 0))
    mat = lambda m, n: pl.BlockSpec((m, n), lambda c: (0, 0))
    out = pl.pallas_call(
        _rwkv_body,
        grid=(nc,),
        in_specs=[pl.BlockSpec((B, C, W_A), lambda c: (0, c, a_blk)),
                  vec(W_A), vec(D_BRANCH), mat(LORA, D_BRANCH), vec(D_BRANCH), mat(LORA, D_BRANCH),
                  vec(D_BRANCH), vec(D_BRANCH), vec(D_BRANCH), vec(D_BRANCH), vec(D_BRANCH),
                  mat(D_BRANCH, D_BRANCH)],
        out_specs=pl.BlockSpec((B, C, D_BRANCH), lambda c: (0, c, 0)),
        out_shape=jax.ShapeDtypeStruct((B, S, D_BRANCH), BF16),
        scratch_shapes=[pltpu.VMEM((B * N_HEADS, HEAD_DIM, HEAD_DIM), F32),
                        pltpu.VMEM((B, 1, W_A), F32),
                        pltpu.VMEM((B * C, D_BRANCH), F32)],
        compiler_params=_params("arbitrary"),
        name="rwkv",
    )(u.reshape(B, S, u.shape[1]), mu, w0, w2, a0, a2, k_k, k_a, r_k, gn_g, gn_b, hsel)
    return out.reshape(T, D_BRANCH)


def _rope(x, cosw, sinw):
    W = x.shape[1]
    lane = lax.broadcasted_iota(I32, (1, W), 1) & (HEAD_DIM - 1)
    half = ROPE_DIM // 2
    partner = jnp.where(lane < half, pltpu.roll(x, W - half, 1), pltpu.roll(x, half, 1))
    return x * cosw + partner * sinw


def _dsa_prep_body(um_ref, ui_ref, cos_ref, sin_ref, qg_ref, kg_ref, hsel_ref,
                   qn_ref, kn_ref, vt_ref, qi_ref, ki_ref, wt_ref):
    DB = D_BRANCH
    cos1, sin1 = cos_ref[...], sin_ref[...]
    cos2, sin2 = jnp.concatenate([cos1] * 2, axis=1), jnp.concatenate([sin1] * 2, axis=1)
    cos4, sin4 = jnp.concatenate([cos2] * 2, axis=1), jnp.concatenate([sin2] * 2, axis=1)
    hsel = hsel_ref[...]

    def headnorm(x, gain):
        ms = _dot_sel(x * x, hsel) * (1.0 / HEAD_DIM)
        return x * lax.rsqrt(ms + NORM_EPS) * gain

    q = _rope(headnorm(um_ref[:, 0:DB].astype(F32), qg_ref[...]), cos4, sin4)
    k = _rope(headnorm(um_ref[:, DB:2 * DB].astype(F32), kg_ref[...]), cos4, sin4)
    qn_ref[...] = (q * (HEAD_DIM ** -0.5 * LOG2E)).astype(BF16)
    kn_ref[...] = k.astype(BF16)
    vt_ref[...] = um_ref[:, 2 * DB:3 * DB].astype(F32).T.astype(BF16)
    qi_ref[...] = _rope(ui_ref[:, 0:IDX_HEADS * IDX_DIM].astype(F32), cos2, sin2).astype(BF16)
    kw = ui_ref[:, IDX_HEADS * IDX_DIM:].astype(F32)
    ki_ref[...] = _rope(kw, cos1, sin1)[:, 0:IDX_DIM].astype(BF16)
    wt_ref[...] = (kw * (IDX_HEADS ** -0.5 * IDX_DIM ** -0.5)).T


def _dsa_prep(u, B, S, cosf, sinf, qg, kg, hsel):
    T = u.shape[0]
    tm = min(512, S)
    ns = S // tm
    vec = lambda n: pl.BlockSpec((1, n), lambda i: (0, 0))
    row = lambda n: pl.BlockSpec((tm, n), lambda i: (i, 0))
    col = lambda n: pl.BlockSpec((None, n, tm), lambda i: (i // ns, 0, i % ns))
    return pl.pallas_call(
        _dsa_prep_body,
        grid=(T // tm,),
        in_specs=[pl.BlockSpec((tm, W_BM), lambda i: (i, OFF_BM // W_BM)),
                  pl.BlockSpec((tm, W_BI), lambda i: (i, OFF_BI // W_BI)),
                  row(LANES), row(LANES), vec(D_BRANCH), vec(D_BRANCH),
                  pl.BlockSpec((D_BRANCH, D_BRANCH), lambda i: (0, 0))],
        out_specs=[row(D_BRANCH), row(D_BRANCH), col(D_BRANCH), row(IDX_HEADS * IDX_DIM),
                   row(IDX_DIM), col(LANES)],
        out_shape=[jax.ShapeDtypeStruct((T, D_BRANCH), BF16),
                   jax.ShapeDtypeStruct((T, D_BRANCH), BF16),
                   jax.ShapeDtypeStruct((B, D_BRANCH, S), BF16),
                   jax.ShapeDtypeStruct((T, IDX_HEADS * IDX_DIM), BF16),
                   jax.ShapeDtypeStruct((T, IDX_DIM), BF16),
                   jax.ShapeDtypeStruct((B, LANES, S), F32)],
        compiler_params=_params("parallel"),
        name="dsa_prep",
    )(u, u, cosf, sinf, qg, kg, hsel)


def _dsa_body(qn_ref, qi_ref, wt_ref, g_ref, kn_ref, vt_ref, ki_ref, o_ref,
              key_ref, khi_ref, klo_ref, m_ref, l_ref, acc_ref, *, ksel, kt):
    QB = qn_ref.shape[0]
    S = kn_ref.shape[0]
    SUB = 8
    q0 = pl.program_id(1) * QB
    qlane = lax.broadcasted_iota(I32, (1, QB), 1)
    lim = (((q0 + qlane) >> CHUNK_SHIFT) + 1) << CHUNK_SHIFT
    n_t = (q0 + QB + kt - 1) // kt
    krow = lax.broadcasted_iota(I32, (kt, 1), 0)
    imin = jnp.int32(INT_MIN)
    qi = qi_ref[...]
    qi_h = [qi[:, h * IDX_DIM:(h + 1) * IDX_DIM] for h in range(IDX_HEADS)]
    w_h = [wt_ref[IDX_DIM + h:IDX_DIM + h + 1, :] for h in range(IDX_HEADS)]

    def fold(x, op):
        return op(x.reshape(kt // SUB, SUB, QB), axis=0)

    def score_tile(t, carry):
        off = pl.multiple_of(t * kt, kt)
        ki_t = ki_ref[pl.ds(off, kt), :]
        sc = jnp.zeros((kt, QB), F32)
        for h in range(IDX_HEADS):
            d = lax.dot_general(ki_t, qi_h[h], (((1,), (1,)), ((), ())),
                                preferred_element_type=F32)
            sc = sc + w_h[h] * jnp.maximum(d, 0.0)
        bits = pltpu.bitcast(sc, I32)
        key = bits ^ ((bits >> 31) & jnp.int32(0x7FFFFFFF))
        key = jnp.where(off + krow < lim, key, imin)
        key_ref[pl.ds(off, kt), :] = key
        khi_ref[pl.ds(off, kt), :] = (key >> 16).astype(I16)
        klo_ref[pl.ds(off, kt), :] = ((key & 0xFFFF) - HALF).astype(I16)
        return carry

    lax.fori_loop(0, n_t, score_tile, 0)

    def count16(ref, pred):
        n_acc = 4

        def body(t, accs):
            off = pl.multiple_of(t * kt, kt)
            accs = list(accs)
            x = ref[pl.ds(off, kt), :]
            for j in range(kt // PACK):
                hit = jnp.where(pred(x[j * PACK:(j + 1) * PACK, :]), jnp.int16(1), jnp.int16(0))
                accs[j % n_acc] = accs[j % n_acc] + hit
            return tuple(accs)

        zero = jnp.zeros((PACK, QB), I16)
        accs = lax.fori_loop(0, n_t, body, (zero,) * n_acc)
        acc = (accs[0] + accs[1]) + (accs[2] + accs[3])
        return jnp.sum(acc.astype(I32), axis=0, keepdims=True)

    def bisect16(ref, target):
        def body(i, ulo):
            ucand = ulo + jnp.left_shift(jnp.int32(1), 15 - i)
            cand = (ucand - HALF).astype(I16)
            cnt = count16(ref, lambda x: x >= cand)
            return jnp.where(cnt >= target, ucand, ulo)

        return lax.fori_loop(0, 16, body, jnp.zeros((1, QB), I32))

    def count(pred):
        def body(t, acc):
            off = pl.multiple_of(t * kt, kt)
            hit = jnp.where(pred(key_ref[pl.ds(off, kt), :], off), 1, 0)
            return acc + fold(hit, jnp.sum)

        acc = lax.fori_loop(0, n_t, body, jnp.zeros((SUB, QB), I32))
        return jnp.sum(acc, axis=0, keepdims=True)

    uhi = bisect16(khi_ref, ksel)
    tau_hi = (uhi - HALF).astype(I16)
    k_rest = ksel - count16(khi_ref, lambda x: x > tau_hi)

    def mask_low(t, carry):
        off = pl.multiple_of(t * kt, kt)
        sl = pl.ds(off, kt)
        klo_ref[sl, :] = jnp.where(khi_ref[sl, :] == tau_hi, klo_ref[sl, :], jnp.int16(-HALF))
        return carry

    lax.fori_loop(0, n_t, mask_low, 0)
    ulo = bisect16(klo_ref, k_rest)
    tau = ((uhi - HALF) << 16) | ulo

    need = ksel - count(lambda key, off: key > tau)
    n_bits = int(np.log2(S))

    def tie_bisect(i, jlo):
        cand = jlo + jnp.left_shift(jnp.int32(1), n_bits - 1 - i)
        cnt = count(lambda key, off: jnp.where(key == tau, off + krow, S) < cand)
        return jnp.where(cnt < need, cand, jlo)

    has_ties = jnp.max(count(lambda key, off: key >= tau)) > ksel
    jmax = lax.cond(has_ties,
                    lambda: lax.fori_loop(0, n_bits, tie_bisect, jnp.zeros((1, QB), I32)),
                    lambda: jnp.full((1, QB), S, I32))

    m_ref[...] = jnp.full_like(m_ref, NEG_BIG)
    l_ref[...] = jnp.zeros_like(l_ref)
    acc_ref[...] = jnp.zeros_like(acc_ref)
    low_half = lax.broadcasted_iota(I32, (1, LANES), 1) < HEAD_DIM
    qn = qn_ref[...].astype(F32)
    q_pair = []
    for h in range(N_HEADS):
        pair = qn[:, (h // 2) * LANES:(h // 2 + 1) * LANES]
        own = low_half if h % 2 == 0 else jnp.logical_not(low_half)
        q_pair.append(jnp.where(own, pair, 0.0).astype(BF16))
    ONES_ROWS = 16
    ones = jnp.ones((ONES_ROWS, kt), BF16)
    tau_c = jnp.maximum(tau, imin + 1)

    def att_tile(t, carry):
        off = pl.multiple_of(t * kt, kt)
        key = key_ref[pl.ds(off, kt), :]
        tie_ok = jnp.where(key == tau_c, off + krow, S) <= jmax
        bias = jnp.where(key > tau_c, 0.0, jnp.where(tie_ok, 0.0, NEG_BIG))
        for g0 in range(0, N_HEADS, HEAD_GROUP):
            heads = range(g0, g0 + HEAD_GROUP)
            pairs = {h // 2: slice((h // 2) * LANES, (h // 2 + 1) * LANES) for h in heads}
            k_p = {pr: kn_ref[pl.ds(off, kt), ps] for pr, ps in pairs.items()}
            vo_p = {pr: jnp.concatenate([vt_ref[ps, pl.ds(off, kt)], ones], axis=0)
                    for pr, ps in pairs.items()}
            s = {h: lax.dot_general(k_p[h // 2], q_pair[h], (((1,), (1,)), ((), ())),
                                    preferred_element_type=F32) + bias for h in heads}
            m_old = {h: m_ref[h] for h in heads}
            m_new = {h: jnp.maximum(m_old[h], jnp.max(fold(s[h], jnp.max), axis=0, keepdims=True))
                     for h in heads}
            p = {h: jnp.exp2(s[h] - m_new[h][0:1, :]).astype(BF16) for h in heads}
            alpha = {h: jnp.exp2(m_old[h] - m_new[h]) for h in heads}
            pvl = {h: jnp.dot(vo_p[h // 2], p[h], preferred_element_type=F32) for h in heads}
            for h in heads:
                l_ref[h] = alpha[h] * l_ref[h] + pvl[h][LANES:LANES + SUB, :]
                m_ref[h] = m_new[h]
                hs = slice(h * HEAD_DIM, (h + 1) * HEAD_DIM)
                own = slice((h % 2) * HEAD_DIM, (h % 2 + 1) * HEAD_DIM)
                acc_ref[hs, :] = alpha[h][0:1, :] * acc_ref[hs, :] + pvl[h][own, :]
        return carry

    lax.fori_loop(0, n_t, att_tile, 0)

    o_t = jnp.concatenate(
        [acc_ref[h * HEAD_DIM:(h + 1) * HEAD_DIM, :] / l_ref[h][0:1, :] for h in range(N_HEADS)],
        axis=0)
    g = g_ref[...].astype(F32)
    o_ref[...] = (o_t.T * (g * jax.nn.sigmoid(g))).astype(o_ref.dtype)


def _dsa(u, qn, kn, vt, qi, ki, wt, B, S, ksel):
    T = u.shape[0]
    QB = min(256, S)
    kt = min(1024, S)
    nq = S // QB
    g_blk = (OFF_BM + 3 * D_BRANCH) // D_BRANCH
    qrow = lambda n: pl.BlockSpec((QB, n), lambda b, i: (b * nq + i, 0))
    full = lambda n: pl.BlockSpec((S, n), lambda b, i: (b, 0))
    return pl.pallas_call(
        functools.partial(_dsa_body, ksel=ksel, kt=kt),
        grid=(B, nq),
        in_specs=[qrow(D_BRANCH), qrow(IDX_HEADS * IDX_DIM),
                  pl.BlockSpec((None, LANES, QB), lambda b, i: (b, 0, i)),
                  pl.BlockSpec((QB, D_BRANCH), lambda b, i: (b * nq + i, g_blk)),
                  full(D_BRANCH),
                  pl.BlockSpec((None, D_BRANCH, S), lambda b, i: (b, 0, 0)),
                  full(IDX_DIM)],
        out_specs=qrow(D_BRANCH),
        out_shape=jax.ShapeDtypeStruct((T, D_BRANCH), BF16),
        scratch_shapes=[pltpu.VMEM((S, QB), I32),
                        pltpu.VMEM((S, QB), I16),
                        pltpu.VMEM((S, QB), I16),
                        pltpu.VMEM((N_HEADS, 8, QB), F32),
                        pltpu.VMEM((N_HEADS, 8, QB), F32),
                        pltpu.VMEM((D_BRANCH, QB), F32)],
        compiler_params=_params("parallel", "arbitrary"),
        name="dsa",
    )(qn, qi, wt, u, kn, vt, ki)


def _lru_body(u_ref, cw_ref, cb_ref, wri_ref, bri_ref, lam_ref, o_ref, xbuf_ref, h_ref):
    TS = u_ref.shape[0]
    DB = D_BRANCH
    PAD = 8

    @pl.when(pl.program_id(1) == 0)
    def _():
        xbuf_ref[0:PAD, :] = jnp.zeros((PAD, DB), F32)
        h_ref[...] = jnp.zeros_like(h_ref)

    x = u_ref[:, 0:DB].astype(F32)
    g = u_ref[:, DB:2 * DB].astype(F32)
    xbuf_ref[PAD:PAD + TS, :] = x
    xc = cb_ref[...]
    for j in range(CONV_WIDTH):
        xc = xc + cw_ref[j:j + 1, :] * xbuf_ref[pl.ds(PAD - (CONV_WIDTH - 1) + j, TS), :]
    xbuf_ref[0:PAD, :] = x[TS - PAD:TS, :]

    ri = jax.nn.sigmoid(_dot(xc, wri_ref[...]) + bri_ref[...])
    r, i = ri[:, 0:DB], ri[:, DB:2 * DB]
    log_a = (-LRU_C) * r * jax.nn.softplus(-lam_ref[...])
    a = jnp.exp(log_a)
    b = jnp.sqrt(jnp.tanh(-log_a) * (1.0 + a * a)) * (i * xc)

    row = lax.broadcasted_iota(I32, (TS, 1), 0)
    step = 1
    while step < TS:
        live = row >= step
        a_s = jnp.where(live, pltpu.roll(a, step, 0), 1.0)
        b_s = jnp.where(live, pltpu.roll(b, step, 0), 0.0)
        b = a * b_s + b
        a = a * a_s
        step *= 2
    h = b + a * h_ref[...]
    h_ref[...] = h[TS - 1:TS, :]
    o_ref[...] = (h * (g * jax.nn.sigmoid(g))).astype(o_ref.dtype)


def _lru(u, B, S, conv_w, conv_b, w_ri, b_ri, lam):
    T = u.shape[0]
    TS = min(256, S)
    ns = S // TS
    vec = lambda n: pl.BlockSpec((1, n), lambda b, s: (0, 0))
    return pl.pallas_call(
        _lru_body,
        grid=(B, ns),
        in_specs=[pl.BlockSpec((TS, W_C), lambda b, s: (b * ns + s, OFF_C // W_C)),
                  pl.BlockSpec((CONV_WIDTH, D_BRANCH), lambda b, s: (0, 0)),
                  vec(D_BRANCH),
                  pl.BlockSpec((D_BRANCH, 2 * D_BRANCH), lambda b, s: (0, 0)),
                  vec(2 * D_BRANCH), vec(D_BRANCH)],
        out_specs=pl.BlockSpec((TS, D_BRANCH), lambda b, s: (b * ns + s, 0)),
        out_shape=jax.ShapeDtypeStruct((T, D_BRANCH), BF16),
        scratch_shapes=[pltpu.VMEM((TS + 8, D_BRANCH), F32), pltpu.VMEM((1, D_BRANCH), F32)],
        compiler_params=_params("parallel", "arbitrary"),
        name="lru",
    )(u, conv_w, conv_b, w_ri, b_ri, lam)


def _merge_body(ya_ref, yb_ref, yc_ref, ug_ref, h_ref, p_ref, wb_ref, wo_ref, wp_ref, wpg_ref,
                o_ref):
    D = D_MODEL
    merged = None
    for n, y_ref in enumerate((ya_ref, yb_ref, yc_ref)):
        term = jax.nn.sigmoid(ug_ref[:, n * D:(n + 1) * D].astype(F32)) * _dot(y_ref[...], wb_ref[n])
        merged = term if merged is None else merged + term
    h1 = h_ref[...] + _dot(merged, wo_ref[...])
    gate = jax.nn.sigmoid(_dot(h1, wpg_ref[...]))
    o_ref[...] = h1 + gate * _dot(p_ref[...], wp_ref[...])


def _merge(ya, yb, yc, u, h, p, wb, wo, wp, wpg):
    T, D = h.shape
    tm = min(512, T)
    row = lambda n: pl.BlockSpec((tm, n), lambda i: (i, 0))
    return pl.pallas_call(
        _merge_body,
        grid=(T // tm,),
        in_specs=[row(D_BRANCH), row(D_BRANCH), row(D_BRANCH),
                  pl.BlockSpec((tm, W_G), lambda i: (i, OFF_G // W_G)),
                  row(D), row(PLE_DIM),
                  pl.BlockSpec((N_BRANCH, D_BRANCH, D), lambda i: (0, 0, 0)),
                  pl.BlockSpec((D, D), lambda i: (0, 0)),
                  pl.BlockSpec((PLE_DIM, D), lambda i: (0, 0)),
                  pl.BlockSpec((D, D), lambda i: (0, 0))],
        out_specs=row(D),
        out_shape=jax.ShapeDtypeStruct((T, D), F32),
        compiler_params=_params("parallel"),
        name="merge",
    )(ya, yb, yc, u, h, p, wb, wo, wp, wpg)


def _reorder_w_in(w):
    a0 = 0
    b0 = a0 + W_A
    c0 = b0 + 3 * D_BRANCH + IDX_HEADS * IDX_DIM + IDX_DIM + IDX_HEADS + D_BRANCH
    g0 = c0 + W_C
    bq = w[:, b0:b0 + 3 * D_BRANCH]
    bi0 = b0 + 3 * D_BRANCH
    n_idx = IDX_HEADS * IDX_DIM + IDX_DIM + IDX_HEADS
    bidx = w[:, bi0:bi0 + n_idx]
    bg = w[:, bi0 + n_idx:c0]
    pad = jnp.zeros((w.shape[0], W_BI - n_idx), w.dtype)
    out = jnp.concatenate([w[:, g0:g0 + W_G], w[:, c0:g0], bq, bg, bidx, pad, w[:, a0:b0]], axis=1)
    return _to_bf16(out)


def _cast_body(x_ref, o_ref):
    o_ref[...] = x_ref[...].astype(o_ref.dtype)


def _to_bf16(w):
    R, N = w.shape
    tn = 512
    return pl.pallas_call(
        _cast_body,
        grid=(N // tn,),
        in_specs=[pl.BlockSpec((R, tn), lambda j: (0, j))],
        out_specs=pl.BlockSpec((R, tn), lambda j: (0, j)),
        out_shape=jax.ShapeDtypeStruct((R, N), BF16),
        compiler_params=_params("parallel"),
        name="cast_bf16",
    )(w)


def _block_diag(w):
    H, n, _ = w.shape
    eye = jnp.eye(H, dtype=w.dtype)
    return (eye[:, None, :, None] * w[:, :, None, :]).reshape(H * n, H * n)


def _rope_tables(positions):
    half = ROPE_DIM // 2
    inv_freq = ROPE_THETA ** (-jnp.arange(0, ROPE_DIM, 2, dtype=F32) / ROPE_DIM)
    ang = positions.astype(F32).reshape(-1, 1) * inv_freq
    cos, sin = jnp.cos(ang), jnp.sin(ang)
    T = cos.shape[0]
    rest = HEAD_DIM - ROPE_DIM
    cos_h = jnp.concatenate([cos, cos, jnp.ones((T, rest), F32)], axis=1)
    sin_h = jnp.concatenate([-sin, sin, jnp.zeros((T, rest), F32)], axis=1)
    return jnp.tile(cos_h, (1, LANES // HEAD_DIM)), jnp.tile(sin_h, (1, LANES // HEAD_DIM))


def kernel(x, p, positions, norm_g, w_in, rwkv_mu, rwkv_w0, rwkv_w2, rwkv_a0, rwkv_a2, rwkv_k_k,
           rwkv_k_a, rwkv_r_k, rwkv_gn_g, rwkv_gn_b, dsa_q_g, dsa_k_g, lru_conv_w, lru_conv_b,
           lru_w_r, lru_b_r, lru_w_i, lru_b_i, lru_lambda, w_branch, w_out, w_ple, w_ple_gate):
    B, S, D = x.shape
    depth = w_in.shape[0]
    T = B * S
    ksel = min(TOPK_MAX, S // 4)
    cosf, sinf = _rope_tables(positions)
    head_id = np.arange(D_BRANCH) // HEAD_DIM
    hsel = jnp.asarray(head_id[:, None] == head_id[None, :], BF16)
    row = lambda v: v.reshape(1, -1)
    per_head = lambda v: jnp.tile(v, N_HEADS).reshape(1, -1)

    h = x.reshape(T, D)
    for i in range(depth):
        u = _inproj(h, row(norm_g[i]), _reorder_w_in(w_in[i]))
        ya = _rwkv(u, B, S, row(rwkv_mu[i]), row(rwkv_w0[i]), rwkv_w2[i], row(rwkv_a0[i]),
                   rwkv_a2[i], row(rwkv_k_k[i]), row(rwkv_k_a[i]), row(rwkv_r_k[i]),
                   row(rwkv_gn_g[i]), row(rwkv_gn_b[i]), hsel)
        qn, kn, vt, qi, ki, wt = _dsa_prep(u, B, S, cosf, sinf, per_head(dsa_q_g[i]),
                                           per_head(dsa_k_g[i]), hsel)
        yb = _dsa(u, qn, kn, vt, qi, ki, wt, B, S, ksel)
        w_ri = jnp.concatenate([_block_diag(lru_w_r[i]), _block_diag(lru_w_i[i])], axis=1)
        b_ri = jnp.concatenate([lru_b_r[i], lru_b_i[i]]).reshape(1, -1)
        yc = _lru(u, B, S, lru_conv_w[i], row(lru_conv_b[i]), w_ri.astype(BF16), b_ri,
                  row(lru_lambda[i]))
        h = _merge(ya, yb, yc, u, h, p[i].reshape(T, PLE_DIM), w_branch[i].astype(BF16),
                   w_out[i].astype(BF16), w_ple[i].astype(BF16), w_ple_gate[i].astype(BF16))
    return h.reshape(B, S, D)
```

```python
import functools

import numpy as np
import jax
import jax.numpy as jnp
from jax import lax
from jax.experimental import pallas as pl
from jax.experimental.pallas import tpu as pltpu

F32 = jnp.float32
BF16 = jnp.bfloat16
I32 = jnp.int32
I16 = jnp.int16
HALF = 1 << 15
PACK = 16
LOG2E = 1.4426950408889634
HEAD_GROUP = 8

D_MODEL = 1024
D_BRANCH = 512
HEAD_DIM = 64
N_HEADS = 8
N_BRANCH = 3
PLE_DIM = 256
CHUNK = 64
CHUNK_SHIFT = 6
NORM_EPS = 1e-6
LORA = 64
RWKV_GN_EPS = 64e-5
IDX_HEADS = 4
IDX_DIM = 64
TOPK_MAX = 256
ROPE_THETA = 500000.0
ROPE_DIM = 16
CONV_WIDTH = 4
LRU_C = 8.0

LANES = 128
VMEM_LIMIT = 48 * 1024 * 1024

W_G = N_BRANCH * D_MODEL
W_C = 2 * D_BRANCH
W_BM = 4 * D_BRANCH
W_BI = 384
W_A = 4 * D_BRANCH + 2 * LORA
OFF_G, OFF_C, OFF_BM, OFF_BI, OFF_A = 0, 3072, 4096, 6144, 6528
D_IN_PAD = OFF_A + W_A

NEG_BIG = -1e30
INT_MIN = -2 ** 31


def _dot(a, b):
    return jnp.dot(a.astype(BF16), b.astype(BF16), preferred_element_type=F32)


def _dot_nt(a, b):
    return lax.dot_general(a.astype(BF16), b.astype(BF16), (((1,), (1,)), ((), ())),
                           preferred_element_type=F32)


SMALL_DOT_PIECES = 1


def _pieces(x):
    hi = x.astype(BF16)
    if SMALL_DOT_PIECES == 1:
        return [hi]
    return [hi, (x - hi.astype(F32)).astype(BF16)]


def _sdot_general(a, b, dims):
    pa, pb = _pieces(a), _pieces(b)
    out = None
    for i, x in enumerate(pa):
        for j, y in enumerate(pb):
            if i + j < SMALL_DOT_PIECES:
                t = lax.dot_general(x, y, (dims, ((), ())), preferred_element_type=F32)
                out = t if out is None else out + t
    return out


def _sdot(a, b):
    return _sdot_general(a, b, ((1,), (0,)))


def _sdot_nt(a, b):
    return _sdot_general(a, b, ((1,), (1,)))


def _dot_sel(x, sel_bf16):
    hi = x.astype(BF16)
    r1 = x - hi.astype(F32)
    mid = r1.astype(BF16)
    lo = (r1 - mid.astype(F32)).astype(BF16)
    out = jnp.dot(hi, sel_bf16, preferred_element_type=F32)
    out = out + jnp.dot(mid, sel_bf16, preferred_element_type=F32)
    return out + jnp.dot(lo, sel_bf16, preferred_element_type=F32)


def _params(*sem):
    return pltpu.CompilerParams(dimension_semantics=sem, vmem_limit_bytes=VMEM_LIMIT)


def _inproj_body(h_ref, g_ref, w_ref, o_ref, hn_ref):
    @pl.when(pl.program_id(1) == 0)
    def _():
        x = h_ref[...]
        ms = jnp.mean(x * x, axis=-1, keepdims=True)
        hn_ref[...] = (x * lax.rsqrt(ms + NORM_EPS) * g_ref[...]).astype(BF16)

    o_ref[...] = jnp.dot(hn_ref[...], w_ref[...], preferred_element_type=F32).astype(o_ref.dtype)


def _inproj(h, g, w):
    T, D = h.shape
    N = w.shape[1]
    tm, tn = min(2048, T), 512
    return pl.pallas_call(
        _inproj_body,
        grid=(T // tm, N // tn),
        in_specs=[pl.BlockSpec((tm, D), lambda i, j: (i, 0)),
                  pl.BlockSpec((1, D), lambda i, j: (0, 0)),
                  pl.BlockSpec((D, tn), lambda i, j: (0, j))],
        out_specs=pl.BlockSpec((tm, tn), lambda i, j: (i, j)),
        out_shape=jax.ShapeDtypeStruct((T, N), BF16),
        scratch_shapes=[pltpu.VMEM((tm, D), BF16)],
        compiler_params=_params("parallel", "arbitrary"),
        name="inproj",
    )(h, g, w)


def _rwkv_body(ua_ref, mu_ref, w0_ref, w2_ref, a0_ref, a2_ref, kk_ref, ka_ref, rk_ref,
               gng_ref, gnb_ref, hsel_ref, o_ref, z_ref, prev_ref, obuf_ref):
    NB, C, _ = ua_ref.shape
    R = NB * C
    DB = D_BRANCH

    @pl.when(pl.program_id(0) == 0)
    def _():
        z_ref[...] = jnp.zeros_like(z_ref)
        prev_ref[...] = jnp.zeros_like(prev_ref)

    ua = ua_ref[...].astype(F32).reshape(R, ua_ref.shape[2])
    row = lax.broadcasted_iota(I32, (R, 1), 0)
    shifted = pltpu.roll(ua, 1, 0)
    for bi in range(NB):
        shifted = jnp.where(row == bi * C, prev_ref[bi], shifted)
    for bi in range(NB):
        prev_ref[bi] = ua[(bi + 1) * C - 1:(bi + 1) * C, :]
    xs = ua + (shifted - ua) * mu_ref[...]

    r = xs[:, 0:DB]
    k = xs[:, DB:2 * DB]
    v = xs[:, 2 * DB:3 * DB]
    wl = xs[:, 3 * DB:3 * DB + LORA]
    al = xs[:, 3 * DB + LORA:3 * DB + 2 * LORA]
    g = xs[:, 3 * DB + 2 * LORA:]
    hsel = hsel_ref[...]

    w_pre = w0_ref[...] + _dot(jnp.tanh(wl), w2_ref[...])
    lw = -jnp.exp(-jax.nn.softplus(-w_pre) - 0.5)
    a = jax.nn.sigmoid(a0_ref[...] + _dot(al, a2_ref[...]))
    kk = k * kk_ref[...]
    k_mod = k * (1.0 + (a - 1.0) * ka_ref[...])
    sums = _dot_sel(jnp.concatenate([kk * kk, r * k_mod * rk_ref[...]], axis=0), hsel)
    kk = kk / jnp.maximum(jnp.sqrt(sums[0:R, :]), 1e-12)
    bonus = sums[R:2 * R, :] * v
    b = kk * a

    ri = lax.broadcasted_iota(I32, (R, R), 0)
    rj = lax.broadcasted_iota(I32, (R, R), 1)
    same_chunk = (ri >> CHUNK_SHIFT) == (rj >> CHUNK_SHIFT)
    cum_sel = jnp.where(same_chunk, jnp.where(ri >= rj, 1.0, 0.0), 0.0).astype(BF16)
    cl = _dot_sel_lhs(cum_sel, lw)
    cl_end = jnp.concatenate(
        [jnp.broadcast_to(cl[(bi + 1) * C - 1:(bi + 1) * C, :], (C, DB)) for bi in range(NB)], axis=0)
    g_incl = jnp.exp(cl)
    g_inv = jnp.exp(-cl)
    d_end = jnp.exp(cl_end - cl)
    kt_all = kk * jnp.exp(cl - lw)
    rt_all = r * g_incl
    bs_all = b * g_inv
    ks_all = k_mod * g_inv
    bh_all = b * d_end
    kh_all = k_mod * d_end

    ci = lax.broadcasted_iota(I32, (C, C), 0)
    cj = lax.broadcasted_iota(I32, (C, C), 1)
    incl = ci >= cj
    strict = ci > cj
    eye = ci == cj
    ident = jnp.where(eye, 1.0, 0.0)
    n_sq = int(np.log2(C)) - 1
    rows = [slice(bi * C, (bi + 1) * C) for bi in range(NB)]
    bh_t = [bh_all[rb, :].T for rb in rows]
    kh_t = [kh_all[rb, :].T for rb in rows]
    units = [(bi, h) for bi in range(NB) for h in range(N_HEADS)]
    un = range(len(units))
    hsl = [slice(h * HEAD_DIM, (h + 1) * HEAD_DIM) for _, h in units]
    rsl = [rows[bi] for bi, _ in units]
    kt = [kt_all[rsl[u], hsl[u]] for u in un]
    rt = [rt_all[rsl[u], hsl[u]] for u in un]
    vh = [v[rsl[u], hsl[u]] for u in un]
    gram = [_sdot_nt(jnp.concatenate([kt[u], rt[u]], axis=0),
                     jnp.concatenate([bs_all[rsl[u], hsl[u]], ks_all[rsl[u], hsl[u]]], axis=0))
            for u in un]
    b_m = [jnp.where(strict, gram[u][0:C, C:2 * C], 0.0) for u in un]
    ab = [jnp.where(incl, gram[u][C:2 * C, 0:C], 0.0) for u in un]
    bk = [jnp.where(incl, gram[u][C:2 * C, C:2 * C], 0.0) for u in un]
    pw = [jnp.where(strict, -gram[u][0:C, 0:C], 0.0) for u in un]
    tinv = [ident + pw[u] for u in un]
    bmv = [_sdot(b_m[u], vh[u]) for u in un]
    nv = [_sdot(jnp.concatenate([kh_t[units[u][0]][hsl[u], :], bk[u]], axis=0), vh[u]) for u in un]
    for _ in range(n_sq):
        pw = [_sdot(pw[u], pw[u]) for u in un]
        tinv = [_sdot(tinv[u], ident + pw[u]) for u in un]
    wu = [-_sdot(tinv[u], jnp.concatenate([kt[u], bmv[u]], axis=1)) for u in un]
    mq = [_sdot(jnp.concatenate([bh_t[units[u][0]][hsl[u], :], ab[u]], axis=0), wu[u]) for u in un]
    lhs = []
    for u in un:
        g_end = g_incl[rsl[u].stop - 1:rsl[u].stop, hsl[u]]
        m_m = jnp.where(eye, g_end, 0.0) + mq[u][0:HEAD_DIM, 0:HEAD_DIM]
        qe = rt[u] + mq[u][HEAD_DIM:, 0:HEAD_DIM]
        lhs.append(jnp.concatenate([qe, m_m], axis=0))
    oz = [_sdot(lhs[u], z_ref[u]) for u in un]
    for u in un:
        n_m = mq[u][0:HEAD_DIM, HEAD_DIM:] + nv[u][0:HEAD_DIM, :]
        o0 = mq[u][HEAD_DIM:, HEAD_DIM:] + nv[u][HEAD_DIM:, :]
        obuf_ref[rsl[u], hsl[u]] = oz[u][0:C, :] + o0
        z_ref[u] = oz[u][C:, :] + n_m

    o = obuf_ref[...]
    inv_n = 1.0 / HEAD_DIM
    mean = _dot_sel(o, hsel) * inv_n
    oc = o - mean
    var = _dot_sel(oc * oc, hsel) * inv_n
    o = oc * lax.rsqrt(var + RWKV_GN_EPS) * gng_ref[...] + gnb_ref[...]
    y = (o + bonus) * (g * jax.nn.sigmoid(g))
    o_ref[...] = y.reshape(NB, C, DB).astype(o_ref.dtype)


def _dot_sel_lhs(sel_bf16, x):
    hi = x.astype(BF16)
    r1 = x - hi.astype(F32)
    mid = r1.astype(BF16)
    lo = (r1 - mid.astype(F32)).astype(BF16)
    out = jnp.dot(sel_bf16, hi, preferred_element_type=F32)
    out = out + jnp.dot(sel_bf16, mid, preferred_element_type=F32)
    return out + jnp.dot(sel_bf16, lo, preferred_element_type=F32)


def _rwkv(u, B, S, mu, w0, w2, a0, a2, k_k, k_a, r_k, gn_g, gn_b, hsel):
    T = u.shape[0]
    C = CHUNK
    nc = S // C
    a_blk = OFF_A // W_A
    vec = lambda n: pl.BlockSpec((1, n), lambda c: (0, 0))
    mat = lambda m, n: pl.BlockSpec((m, n), lambda c: (0, 0))
    out = pl.pallas_call(
        _rwkv_body,
        grid=(nc,),
        in_specs=[pl.BlockSpec((B, C, W_A), lambda c: (0, c, a_blk)),
                  vec(W_A), vec(D_BRANCH), mat(LORA, D_BRANCH), vec(D_BRANCH), mat(LORA, D_BRANCH),
                  vec(D_BRANCH), vec(D_BRANCH), vec(D_BRANCH), vec(D_BRANCH), vec(D_BRANCH),
                  mat(D_BRANCH, D_BRANCH)],
        out_specs=pl.BlockSpec((B, C, D_BRANCH), lambda c: (0, c, 0)),
        out_shape=jax.ShapeDtypeStruct((B, S, D_BRANCH), BF16),
        scratch_shapes=[pltpu.VMEM((B * N_HEADS, HEAD_DIM, HEAD_DIM), F32),
                        pltpu.VMEM((B, 1, W_A), F32),
                        pltpu.VMEM((B * C, D_BRANCH), F32)],
        compiler_params=_params("arbitrary"),
        name="rwkv",
    )(u.reshape(B, S, u.shape[1]), mu, w0, w2, a0, a2, k_k, k_a, r_k, gn_g, gn_b, hsel)
    return out.reshape(T, D_BRANCH)


def _rope(x, cosw, sinw):
    W = x.shape[1]
    lane = lax.broadcasted_iota(I32, (1, W), 1) & (HEAD_DIM - 1)
    half = ROPE_DIM // 2
    partner = jnp.where(lane < half, pltpu.roll(x, W - half, 1), pltpu.roll(x, half, 1))
    return x * cosw + partner * sinw


def _dsa_prep_body(um_ref, ui_ref, cos_ref, sin_ref, qg_ref, kg_ref, hsel_ref,
                   qn_ref, kn_ref, vt_ref, qi_ref, ki_ref, wt_ref):
    DB = D_BRANCH
    cos1, sin1 = cos_ref[...], sin_ref[...]
    cos2, sin2 = jnp.concatenate([cos1] * 2, axis=1), jnp.concatenate([sin1] * 2, axis=1)
    cos4, sin4 = jnp.concatenate([cos2] * 2, axis=1), jnp.concatenate([sin2] * 2, axis=1)
    hsel = hsel_ref[...]

    def headnorm(x, gain):
        ms = _dot_sel(x * x, hsel) * (1.0 / HEAD_DIM)
        return x * lax.rsqrt(ms + NORM_EPS) * gain

    q = _rope(headnorm(um_ref[:, 0:DB].astype(F32), qg_ref[...]), cos4, sin4)
    k = _rope(headnorm(um_ref[:, DB:2 * DB].astype(F32), kg_ref[...]), cos4, sin4)
    qn_ref[...] = (q * (HEAD_DIM ** -0.5 * LOG2E)).astype(BF16)
    kn_ref[...] = k.astype(BF16)
    vt_ref[...] = um_ref[:, 2 * DB:3 * DB].astype(F32).T.astype(BF16)
    qi_ref[...] = _rope(ui_ref[:, 0:IDX_HEADS * IDX_DIM].astype(F32), cos2, sin2).astype(BF16)
    kw = ui_ref[:, IDX_HEADS * IDX_DIM:].astype(F32)
    ki_ref[...] = _rope(kw, cos1, sin1)[:, 0:IDX_DIM].astype(BF16)
    wt_ref[...] = (kw * (IDX_HEADS ** -0.5 * IDX_DIM ** -0.5)).T


def _dsa_prep(u, B, S, cosf, sinf, qg, kg, hsel):
    T = u.shape[0]
    tm = min(512, S)
    ns = S // tm
    vec = lambda n: pl.BlockSpec((1, n), lambda i: (0, 0))
    row = lambda n: pl.BlockSpec((tm, n), lambda i: (i, 0))
    col = lambda n: pl.BlockSpec((None, n, tm), lambda i: (i // ns, 0, i % ns))
    return pl.pallas_call(
        _dsa_prep_body,
        grid=(T // tm,),
        in_specs=[pl.BlockSpec((tm, W_BM), lambda i: (i, OFF_BM // W_BM)),
                  pl.BlockSpec((tm, W_BI), lambda i: (i, OFF_BI // W_BI)),
                  row(LANES), row(LANES), vec(D_BRANCH), vec(D_BRANCH),
                  pl.BlockSpec((D_BRANCH, D_BRANCH), lambda i: (0, 0))],
        out_specs=[row(D_BRANCH), row(D_BRANCH), col(D_BRANCH), row(IDX_HEADS * IDX_DIM),
                   row(IDX_DIM), col(LANES)],
        out_shape=[jax.ShapeDtypeStruct((T, D_BRANCH), BF16),
                   jax.ShapeDtypeStruct((T, D_BRANCH), BF16),
                   jax.ShapeDtypeStruct((B, D_BRANCH, S), BF16),
                   jax.ShapeDtypeStruct((T, IDX_HEADS * IDX_DIM), BF16),
                   jax.ShapeDtypeStruct((T, IDX_DIM), BF16),
                   jax.ShapeDtypeStruct((B, LANES, S), F32)],
        compiler_params=_params("parallel"),
        name="dsa_prep",
    )(u, u, cosf, sinf, qg, kg, hsel)


def _dsa_body(qn_ref, qi_ref, wt_ref, g_ref, kn_ref, vt_ref, ki_ref, o_ref,
              key_ref, khi_ref, klo_ref, m_ref, l_ref, acc_ref, *, ksel, kt):
    QB = qn_ref.shape[0]
    S = kn_ref.shape[0]
    SUB = 8
    q0 = pl.program_id(1) * QB
    qlane = lax.broadcasted_iota(I32, (1, QB), 1)
    lim = (((q0 + qlane) >> CHUNK_SHIFT) + 1) << CHUNK_SHIFT
    n_t = (q0 + QB + kt - 1) // kt
    krow = lax.broadcasted_iota(I32, (kt, 1), 0)
    imin = jnp.int32(INT_MIN)
    qi = qi_ref[...]
    qi_h = [qi[:, h * IDX_DIM:(h + 1) * IDX_DIM] for h in range(IDX_HEADS)]
    w_h = [wt_ref[IDX_DIM + h:IDX_DIM + h + 1, :] for h in range(IDX_HEADS)]

    def fold(x, op):
        return op(x.reshape(kt // SUB, SUB, QB), axis=0)

    def score_tile(t, carry):
        off = pl.multiple_of(t * kt, kt)
        ki_t = ki_ref[pl.ds(off, kt), :]
        sc = jnp.zeros((kt, QB), F32)
        for h in range(IDX_HEADS):
            d = lax.dot_general(ki_t, qi_h[h], (((1,), (1,)), ((), ())),
                                preferred_element_type=F32)
            sc = sc + w_h[h] * jnp.maximum(d, 0.0)
        bits = pltpu.bitcast(sc, I32)
        key = bits ^ ((bits >> 31) & jnp.int32(0x7FFFFFFF))
        key = jnp.where(off + krow < lim, key, imin)
        key_ref[pl.ds(off, kt), :] = key
        khi_ref[pl.ds(off, kt), :] = (key >> 16).astype(I16)
        klo_ref[pl.ds(off, kt), :] = ((key & 0xFFFF) - HALF).astype(I16)
        return carry

    lax.fori_loop(0, n_t, score_tile, 0)

    def count16(ref, pred):
        n_acc = 4

        def body(t, accs):
            off = pl.multiple_of(t * kt, kt)
            accs = list(accs)
            x = ref[pl.ds(off, kt), :]
            for j in range(kt // PACK):
                hit = jnp.where(pred(x[j * PACK:(j + 1) * PACK, :]), jnp.int16(1), jnp.int16(0))
                accs[j % n_acc] = accs[j % n_acc] + hit
            return tuple(accs)

        zero = jnp.zeros((PACK, QB), I16)
        accs = lax.fori_loop(0, n_t, body, (zero,) * n_acc)
        acc = (accs[0] + accs[1]) + (accs[2] + accs[3])
        return jnp.sum(acc.astype(I32), axis=0, keepdims=True)

    def bisect16(ref, target):
        def body(i, ulo):
            ucand = ulo + jnp.left_shift(jnp.int32(1), 15 - i)
            cand = (ucand - HALF).astype(I16)
            cnt = count16(ref, lambda x: x >= cand)
            return jnp.where(cnt >= target, ucand, ulo)

        return lax.fori_loop(0, 16, body, jnp.zeros((1, QB), I32))

    uhi = bisect16(khi_ref, ksel)
    tau_hi = (uhi - HALF).astype(I16)
    k_rest = ksel - count16(khi_ref, lambda x: x > tau_hi)

    def mask_low(t, carry):
        off = pl.multiple_of(t * kt, kt)
        sl = pl.ds(off, kt)
        klo_ref[sl, :] = jnp.where(khi_ref[sl, :] == tau_hi, klo_ref[sl, :], jnp.int16(-HALF))
        return carry

    lax.fori_loop(0, n_t, mask_low, 0)
    ulo = bisect16(klo_ref, k_rest)
    tau = ((uhi - HALF) << 16) | ulo

    tau_lo = (ulo - HALF).astype(I16)
    need = k_rest - count16(klo_ref, lambda x: x > tau_lo)
    n_bits = int(np.log2(S))
    row32 = lax.broadcasted_iota(I32, (kt, QB), 0)

    def tie_index(t, carry):
        off = pl.multiple_of(t * kt, kt)
        sl = pl.ds(off, kt)
        idx = (row32 + off).astype(I16)
        at_tau = jnp.where(klo_ref[sl, :] == tau_lo, idx, jnp.int16(S))
        klo_ref[sl, :] = jnp.where(khi_ref[sl, :] == tau_hi, at_tau, jnp.int16(S))
        return carry

    lax.fori_loop(0, n_t, tie_index, 0)

    def tie_bisect(i, jlo):
        cand = jlo + jnp.left_shift(jnp.int32(1), n_bits - 1 - i)
        cand16 = cand.astype(I16)
        cnt = count16(klo_ref, lambda x: x < cand16)
        return jnp.where(cnt < need, cand, jlo)

    n_eq = count16(klo_ref, lambda x: x < jnp.int16(S))
    has_ties = jnp.max(jnp.where(tau == imin, 0, n_eq - need)) > 0
    jmax = lax.cond(has_ties,
                    lambda: lax.fori_loop(0, n_bits, tie_bisect, jnp.zeros((1, QB), I32)),
                    lambda: jnp.full((1, QB), S - 1, I32))

    m_ref[...] = jnp.full_like(m_ref, NEG_BIG)
    l_ref[...] = jnp.zeros_like(l_ref)
    acc_ref[...] = jnp.zeros_like(acc_ref)
    low_half = lax.broadcasted_iota(I32, (1, LANES), 1) < HEAD_DIM
    qn = qn_ref[...].astype(F32)
    q_pair = []
    for h in range(N_HEADS):
        pair = qn[:, (h // 2) * LANES:(h // 2 + 1) * LANES]
        own = low_half if h % 2 == 0 else jnp.logical_not(low_half)
        q_pair.append(jnp.where(own, pair, 0.0).astype(BF16))
    ONES_ROWS = 16
    ones = jnp.ones((ONES_ROWS, kt), BF16)
    tau_c = jnp.maximum(tau, imin + 1)

    def att_tile(t, carry):
        off = pl.multiple_of(t * kt, kt)
        key = key_ref[pl.ds(off, kt), :]
        tie_ok = jnp.where(key == tau_c, off + krow, S) <= jmax
        bias = jnp.where(key > tau_c, 0.0, jnp.where(tie_ok, 0.0, NEG_BIG))
        for g0 in range(0, N_HEADS, HEAD_GROUP):
            heads = range(g0, g0 + HEAD_GROUP)
            pairs = {h // 2: slice((h // 2) * LANES, (h // 2 + 1) * LANES) for h in heads}
            k_p = {pr: kn_ref[pl.ds(off, kt), ps] for pr, ps in pairs.items()}
            vo_p = {pr: jnp.concatenate([vt_ref[ps, pl.ds(off, kt)], ones], axis=0)
                    for pr, ps in pairs.items()}
            s = {h: lax.dot_general(k_p[h // 2], q_pair[h], (((1,), (1,)), ((), ())),
                                    preferred_element_type=F32) + bias for h in heads}
            m_old = {h: m_ref[h] for h in heads}
            m_new = {h: jnp.maximum(m_old[h], jnp.max(fold(s[h], jnp.max), axis=0, keepdims=True))
                     for h in heads}
            p = {h: jnp.exp2(s[h] - m_new[h][0:1, :]).astype(BF16) for h in heads}
            alpha = {h: jnp.exp2(m_old[h] - m_new[h]) for h in heads}
            pvl = {h: jnp.dot(vo_p[h // 2], p[h], preferred_element_type=F32) for h in heads}
            for h in heads:
                l_ref[h] = alpha[h] * l_ref[h] + pvl[h][LANES:LANES + SUB, :]
                m_ref[h] = m_new[h]
                hs = slice(h * HEAD_DIM, (h + 1) * HEAD_DIM)
                own = slice((h % 2) * HEAD_DIM, (h % 2 + 1) * HEAD_DIM)
                acc_ref[hs, :] = alpha[h][0:1, :] * acc_ref[hs, :] + pvl[h][own, :]
        return carry

    lax.fori_loop(0, n_t, att_tile, 0)

    o_t = jnp.concatenate(
        [acc_ref[h * HEAD_DIM:(h + 1) * HEAD_DIM, :] / l_ref[h][0:1, :] for h in range(N_HEADS)],
        axis=0)
    g = g_ref[...].astype(F32)
    o_ref[...] = (o_t.T * (g * jax.nn.sigmoid(g))).astype(o_ref.dtype)


def _dsa(u, qn, kn, vt, qi, ki, wt, B, S, ksel):
    T = u.shape[0]
    QB = min(256, S)
    kt = min(1024, S)
    nq = S // QB
    g_blk = (OFF_BM + 3 * D_BRANCH) // D_BRANCH
    qrow = lambda n: pl.BlockSpec((QB, n), lambda b, i: (b * nq + i, 0))
    full = lambda n: pl.BlockSpec((S, n), lambda b, i: (b, 0))
    return pl.pallas_call(
        functools.partial(_dsa_body, ksel=ksel, kt=kt),
        grid=(B, nq),
        in_specs=[qrow(D_BRANCH), qrow(IDX_HEADS * IDX_DIM),
                  pl.BlockSpec((None, LANES, QB), lambda b, i: (b, 0, i)),
                  pl.BlockSpec((QB, D_BRANCH), lambda b, i: (b * nq + i, g_blk)),
                  full(D_BRANCH),
                  pl.BlockSpec((None, D_BRANCH, S), lambda b, i: (b, 0, 0)),
                  full(IDX_DIM)],
        out_specs=qrow(D_BRANCH),
        out_shape=jax.ShapeDtypeStruct((T, D_BRANCH), BF16),
        scratch_shapes=[pltpu.VMEM((S, QB), I32),
                        pltpu.VMEM((S, QB), I16),
                        pltpu.VMEM((S, QB), I16),
                        pltpu.VMEM((N_HEADS, 8, QB), F32),
                        pltpu.VMEM((N_HEADS, 8, QB), F32),
                        pltpu.VMEM((D_BRANCH, QB), F32)],
        compiler_params=_params("parallel", "arbitrary"),
        name="dsa",
    )(qn, qi, wt, u, kn, vt, ki)


def _lru_body(u_ref, cw_ref, cb_ref, wri_ref, bri_ref, lam_ref, o_ref, xbuf_ref, h_ref):
    TS = u_ref.shape[0]
    DB = D_BRANCH
    PAD = 8

    @pl.when(pl.program_id(1) == 0)
    def _():
        xbuf_ref[0:PAD, :] = jnp.zeros((PAD, DB), F32)
        h_ref[...] = jnp.zeros_like(h_ref)

    x = u_ref[:, 0:DB].astype(F32)
    g = u_ref[:, DB:2 * DB].astype(F32)
    xbuf_ref[PAD:PAD + TS, :] = x
    xc = cb_ref[...]
    for j in range(CONV_WIDTH):
        xc = xc + cw_ref[j:j + 1, :] * xbuf_ref[pl.ds(PAD - (CONV_WIDTH - 1) + j, TS), :]
    xbuf_ref[0:PAD, :] = x[TS - PAD:TS, :]

    ri = jax.nn.sigmoid(_dot(xc, wri_ref[...]) + bri_ref[...])
    r, i = ri[:, 0:DB], ri[:, DB:2 * DB]
    log_a = (-LRU_C) * r * jax.nn.softplus(-lam_ref[...])
    a = jnp.exp(log_a)
    b = jnp.sqrt(jnp.tanh(-log_a) * (1.0 + a * a)) * (i * xc)

    row = lax.broadcasted_iota(I32, (TS, 1), 0)
    step = 1
    while step < TS:
        live = row >= step
        a_s = jnp.where(live, pltpu.roll(a, step, 0), 1.0)
        b_s = jnp.where(live, pltpu.roll(b, step, 0), 0.0)
        b = a * b_s + b
        a = a * a_s
        step *= 2
    h = b + a * h_ref[...]
    h_ref[...] = h[TS - 1:TS, :]
    o_ref[...] = (h * (g * jax.nn.sigmoid(g))).astype(o_ref.dtype)


def _lru(u, B, S, conv_w, conv_b, w_ri, b_ri, lam):
    T = u.shape[0]
    TS = min(256, S)
    ns = S // TS
    vec = lambda n: pl.BlockSpec((1, n), lambda b, s: (0, 0))
    return pl.pallas_call(
        _lru_body,
        grid=(B, ns),
        in_specs=[pl.BlockSpec((TS, W_C), lambda b, s: (b * ns + s, OFF_C // W_C)),
                  pl.BlockSpec((CONV_WIDTH, D_BRANCH), lambda b, s: (0, 0)),
                  vec(D_BRANCH),
                  pl.BlockSpec((D_BRANCH, 2 * D_BRANCH), lambda b, s: (0, 0)),
                  vec(2 * D_BRANCH), vec(D_BRANCH)],
        out_specs=pl.BlockSpec((TS, D_BRANCH), lambda b, s: (b * ns + s, 0)),
        out_shape=jax.ShapeDtypeStruct((T, D_BRANCH), BF16),
        scratch_shapes=[pltpu.VMEM((TS + 8, D_BRANCH), F32), pltpu.VMEM((1, D_BRANCH), F32)],
        compiler_params=_params("parallel", "arbitrary"),
        name="lru",
    )(u, conv_w, conv_b, w_ri, b_ri, lam)


def _merge_body(ya_ref, yb_ref, yc_ref, ug_ref, h_ref, p_ref, wb_ref, wo_ref, wp_ref, wpg_ref,
                o_ref):
    D = D_MODEL
    merged = None
    for n, y_ref in enumerate((ya_ref, yb_ref, yc_ref)):
        term = jax.nn.sigmoid(ug_ref[:, n * D:(n + 1) * D].astype(F32)) * _dot(y_ref[...], wb_ref[n])
        merged = term if merged is None else merged + term
    h1 = h_ref[...] + _dot(merged, wo_ref[...])
    gate = jax.nn.sigmoid(_dot(h1, wpg_ref[...]))
    o_ref[...] = h1 + gate * _dot(p_ref[...], wp_ref[...])


def _merge(ya, yb, yc, u, h, p, wb, wo, wp, wpg):
    T, D = h.shape
    tm = min(512, T)
    row = lambda n: pl.BlockSpec((tm, n), lambda i: (i, 0))
    return pl.pallas_call(
        _merge_body,
        grid=(T // tm,),
        in_specs=[row(D_BRANCH), row(D_BRANCH), row(D_BRANCH),
                  pl.BlockSpec((tm, W_G), lambda i: (i, OFF_G // W_G)),
                  row(D), row(PLE_DIM),
                  pl.BlockSpec((N_BRANCH, D_BRANCH, D), lambda i: (0, 0, 0)),
                  pl.BlockSpec((D, D), lambda i: (0, 0)),
                  pl.BlockSpec((PLE_DIM, D), lambda i: (0, 0)),
                  pl.BlockSpec((D, D), lambda i: (0, 0))],
        out_specs=row(D),
        out_shape=jax.ShapeDtypeStruct((T, D), F32),
        compiler_params=_params("parallel"),
        name="merge",
    )(ya, yb, yc, u, h, p, wb, wo, wp, wpg)


def _reorder_w_in(w):
    a0 = 0
    b0 = a0 + W_A
    c0 = b0 + 3 * D_BRANCH + IDX_HEADS * IDX_DIM + IDX_DIM + IDX_HEADS + D_BRANCH
    g0 = c0 + W_C
    bq = w[:, b0:b0 + 3 * D_BRANCH]
    bi0 = b0 + 3 * D_BRANCH
    n_idx = IDX_HEADS * IDX_DIM + IDX_DIM + IDX_HEADS
    bidx = w[:, bi0:bi0 + n_idx]
    bg = w[:, bi0 + n_idx:c0]
    pad = jnp.zeros((w.shape[0], W_BI - n_idx), w.dtype)
    out = jnp.concatenate([w[:, g0:g0 + W_G], w[:, c0:g0], bq, bg, bidx, pad, w[:, a0:b0]], axis=1)
    return _to_bf16(out)


def _cast_body(x_ref, o_ref):
    o_ref[...] = x_ref[...].astype(o_ref.dtype)


def _to_bf16(w):
    R, N = w.shape
    tn = 512
    return pl.pallas_call(
        _cast_body,
        grid=(N // tn,),
        in_specs=[pl.BlockSpec((R, tn), lambda j: (0, j))],
        out_specs=pl.BlockSpec((R, tn), lambda j: (0, j)),
        out_shape=jax.ShapeDtypeStruct((R, N), BF16),
        compiler_params=_params("parallel"),
        name="cast_bf16",
    )(w)


def _block_diag(w):
    H, n, _ = w.shape
    eye = jnp.eye(H, dtype=w.dtype)
    return (eye[:, None, :, None] * w[:, :, None, :]).reshape(H * n, H * n)


def _rope_tables(positions):
    half = ROPE_DIM // 2
    inv_freq = ROPE_THETA ** (-jnp.arange(0, ROPE_DIM, 2, dtype=F32) / ROPE_DIM)
    ang = positions.astype(F32).reshape(-1, 1) * inv_freq
    cos, sin = jnp.cos(ang), jnp.sin(ang)
    T = cos.shape[0]
    rest = HEAD_DIM - ROPE_DIM
    cos_h = jnp.concatenate([cos, cos, jnp.ones((T, rest), F32)], axis=1)
    sin_h = jnp.concatenate([-sin, sin, jnp.zeros((T, rest), F32)], axis=1)
    return jnp.tile(cos_h, (1, LANES // HEAD_DIM)), jnp.tile(sin_h, (1, LANES // HEAD_DIM))


def kernel(x, p, positions, norm_g, w_in, rwkv_mu, rwkv_w0, rwkv_w2, rwkv_a0, rwkv_a2, rwkv_k_k,
           rwkv_k_a, rwkv_r_k, rwkv_gn_g, rwkv_gn_b, dsa_q_g, dsa_k_g, lru_conv_w, lru_conv_b,
           lru_w_r, lru_b_r, lru_w_i, lru_b_i, lru_lambda, w_branch, w_out, w_ple, w_ple_gate):
    B, S, D = x.shape
    depth = w_in.shape[0]
    T = B * S
    ksel = min(TOPK_MAX, S // 4)
    cosf, sinf = _rope_tables(positions)
    head_id = np.arange(D_BRANCH) // HEAD_DIM
    hsel = jnp.asarray(head_id[:, None] == head_id[None, :], BF16)
    row = lambda v: v.reshape(1, -1)
    per_head = lambda v: jnp.tile(v, N_HEADS).reshape(1, -1)

    h = x.reshape(T, D)
    for i in range(depth):
        u = _inproj(h, row(norm_g[i]), _reorder_w_in(w_in[i]))
        ya = _rwkv(u, B, S, row(rwkv_mu[i]), row(rwkv_w0[i]), rwkv_w2[i], row(rwkv_a0[i]),
                   rwkv_a2[i], row(rwkv_k_k[i]), row(rwkv_k_a[i]), row(rwkv_r_k[i]),
                   row(rwkv_gn_g[i]), row(rwkv_gn_b[i]), hsel)
        qn, kn, vt, qi, ki, wt = _dsa_prep(u, B, S, cosf, sinf, per_head(dsa_q_g[i]),
                                           per_head(dsa_k_g[i]), hsel)
        yb = _dsa(u, qn, kn, vt, qi, ki, wt, B, S, ksel)
        w_ri = jnp.concatenate([_block_diag(lru_w_r[i]), _block_diag(lru_w_i[i])], axis=1)
        b_ri = jnp.concatenate([lru_b_r[i], lru_b_i[i]]).reshape(1, -1)
        yc = _lru(u, B, S, lru_conv_w[i], row(lru_conv_b[i]), w_ri.astype(BF16), b_ri,
                  row(lru_lambda[i]))
        h = _merge(ya, yb, yc, u, h, p[i].reshape(T, PLE_DIM), w_branch[i].astype(BF16),
                   w_out[i].astype(BF16), w_ple[i].astype(BF16), w_ple_gate[i].astype(BF16))
    return h.reshape(B, S, D)
```
